```python
import math
import jax, jax.numpy as jnp
from jax import lax
import numpy as np

D_MODEL = 1024
BATCH = 16
SEQ = 2048
DEPTH = 4

GRID_W = 64
CTX_LEN = 256
ATTN_HEADS = 4
ATTN_HEAD_DIM = 64
ATTN_V_DIM = 2 * ATTN_HEAD_DIM
ATTN_WIDTH = ATTN_HEADS * ATTN_V_DIM
CONV_WIDTH = D_MODEL // 4
CONV_K = 3
POOL_WIDTH = D_MODEL // 4
POOL_WINDOWS = (2, 4, 8, 16)
POOL_GROUPS = len(POOL_WINDOWS)
POOL_GROUP_DIM = POOL_WIDTH // POOL_GROUPS
MIX_WIDTH = ATTN_WIDTH + CONV_WIDTH + POOL_WIDTH
D_FF = 2816
ROPE_BASE = 10000.0
Q_BLOCK = 128
N_MOD = 9
EPS = 1e-6
COL_SIZES = (ATTN_WIDTH, ATTN_WIDTH, ATTN_WIDTH, CONV_WIDTH, CONV_WIDTH, CONV_WIDTH, POOL_WIDTH)
COL_SPLITS = tuple(int(v) for v in np.cumsum(COL_SIZES)[:-1])
IN_COLS = sum(COL_SIZES)

kernel_name = "hymba_diffattn_conv_pool_macaron_dit"


def _rms(x, w):
    x32 = x.astype(jnp.float32)
    y = x32 * lax.rsqrt(jnp.mean(x32 * x32, axis=-1, keepdims=True) + EPS)
    return (y * w.astype(jnp.float32)).astype(x.dtype)


def _modulate(x, shift, scale):
    return x * (1 + scale[:, None, :]) + shift[:, None, :]


def _swiglu(x, w_gu, w_down):
    g, u = jnp.split(x @ w_gu, 2, axis=-1)
    return (jax.nn.silu(g) * u) @ w_down


def _axial_rope_tables(n_tokens):
    rows = n_tokens // GRID_W
    row = jnp.repeat(jnp.arange(rows), GRID_W).astype(jnp.float32)
    col = jnp.tile(jnp.arange(GRID_W), rows).astype(jnp.float32)
    nfreq = ATTN_HEAD_DIM // 4
    inv = ROPE_BASE ** (-jnp.arange(nfreq, dtype=jnp.float32) / nfreq)
    ang = jnp.stack([row[:, None] * inv, col[:, None] * inv], axis=1)
    return jnp.cos(ang), jnp.sin(ang)


def _apply_rope(x, cos, sin):
    xs = x.reshape(x.shape[:-1] + (2, 2, -1))
    x1, x2 = xs[..., 0, :], xs[..., 1, :]
    c = cos[None, :, None, None].astype(x.dtype)
    s = sin[None, :, None, None].astype(x.dtype)
    out = jnp.stack([x1 * c - x2 * s, x1 * s + x2 * c], axis=-2)
    return out.reshape(x.shape)


def _diff_attn(q, k, v, lam):
    s = jnp.einsum('bqhid,bkhid->bhiqk', q, k, preferred_element_type=jnp.float32)
    p = jax.nn.softmax(s * (ATTN_HEAD_DIM ** -0.5), axis=-1)
    a = p[:, :, 0] - lam * p[:, :, 1]
    return jnp.einsum('bhqk,bkhe->bqhe', a.astype(v.dtype), v)


def _diff_attn_blocked(q, k, v, lam):
    B, S = q.shape[:2]
    nb = S // Q_BLOCK
    qb = q.reshape((B, nb, Q_BLOCK) + q.shape[2:]).transpose(1, 0, 2, 3, 4, 5)
    out = lax.map(lambda qq: _diff_attn(qq, k, v, lam), qb)
    return out.transpose(1, 0, 2, 3, 4).reshape(B, S, ATTN_HEADS, ATTN_V_DIM)


def _attn_post(o, subln_w, lam_init):
    B, L = o.shape[:2]
    return (_rms(o, subln_w) * (1.0 - lam_init)).reshape(B, L, ATTN_WIDTH)


def _gated_conv(b, cg, xin, w):
    u = cg * xin
    L = u.shape[1]
    up = jnp.pad(u, ((0, 0), (1, 1), (0, 0)))
    return b * (up[:, :L] * w[0] + up[:, 1:L + 1] * w[1] + up[:, 2:] * w[2])


def _multiscale_pool(u, w_grp, scale):
    B, L, _ = u.shape
    ug = u.reshape(B, L, POOL_GROUPS, POOL_GROUP_DIM).astype(jnp.float32)
    csum = jnp.pad(jnp.cumsum(ug, axis=1), ((0, 0), (1, 0), (0, 0), (0, 0)))
    t = jnp.arange(L)
    outs = []
    for g, win in enumerate(POOL_WINDOWS):
        lo = jnp.clip(t - win // 2, 0, L)
        hi = jnp.clip(t + win // 2, 0, L)
        ssum = csum[:, hi, g] - csum[:, lo, g]
        cnt = (hi - lo).astype(jnp.float32)[None, :, None]
        outs.append(ssum / cnt - ug[:, :, g])
    pooled = jnp.stack(outs, axis=2).astype(u.dtype)
    y = jnp.einsum('blgc,gcd->blgd', pooled, w_grp).reshape(B, L, POOL_WIDTH)
    return y * scale


def _hybrid_mixer(u, uc, ctx_out, w_in, w_out, q_norm_w, k_norm_w, lam, lam_init,
                  subln_w, conv_w, pool_w, pool_scale, cos, sin):
    B, S, _ = u.shape
    Lc = uc.shape[1]
    H, dh = ATTN_HEADS, ATTN_HEAD_DIM
    if ctx_out:
        qc, kc, vc, cbc, ccc, cxc, puc = jnp.split(uc @ w_in, COL_SPLITS, axis=-1)
    else:
        kc, vc = jnp.split(uc @ w_in[:, ATTN_WIDTH:3 * ATTN_WIDTH], 2, axis=-1)
    kc = _rms(kc.reshape(B, Lc, H, 2, dh), k_norm_w)
    vc = vc.reshape(B, Lc, H, ATTN_V_DIM)
    q, k, v, cb, cc, cx, pu = jnp.split(u @ w_in, COL_SPLITS, axis=-1)
    q = _apply_rope(_rms(q.reshape(B, S, H, 2, dh), q_norm_w), cos, sin)
    k = _apply_rope(_rms(k.reshape(B, S, H, 2, dh), k_norm_w), cos, sin)
    k_all = jnp.concatenate([k, kc], axis=1)
    v_all = jnp.concatenate([v.reshape(B, S, H, ATTN_V_DIM), vc], axis=1)
    attn = _diff_attn_blocked(q, k_all, v_all, lam)
    y = jnp.concatenate([_attn_post(attn, subln_w, lam_init),
                         _gated_conv(cb, cc, cx, conv_w),
                         _multiscale_pool(pu, pool_w, pool_scale)], axis=-1) @ w_out
    if not ctx_out:
        return y, None
    qc = _rms(qc.reshape(B, Lc, H, 2, dh), q_norm_w)
    attn_c = _diff_attn(qc, kc, vc, lam)
    yc = jnp.concatenate([_attn_post(attn_c, subln_w, lam_init),
                          _gated_conv(cbc, ccc, cxc, conv_w),
                          _multiscale_pool(puc, pool_w, pool_scale)], axis=-1) @ w_out
    return y, yc


def setup_inputs(seed: int = 0) -> dict:
    key = jax.random.key(seed)
    ks = jax.random.split(key, 20)

    def n(k, shape, s):
        return jax.random.normal(k, shape, jnp.float32) * s

    return {
        "x": n(ks[0], (BATCH, SEQ, D_MODEL), 1.0),
        "c": n(ks[1], (BATCH, D_MODEL), 1.0),
        "ctx": n(ks[2], (BATCH, CTX_LEN, D_MODEL), 1.0),
        "c_ctx": n(ks[3], (D_MODEL,), 1.0),
        "norm_w": 1.0 + n(ks[4], (DEPTH, 3, D_MODEL), 0.1),
        "w_mod": n(ks[5], (DEPTH, D_MODEL, N_MOD * D_MODEL), 0.5 * D_MODEL ** -0.5),
        "b_mod": n(ks[6], (DEPTH, N_MOD * D_MODEL), 0.02),
        "ffn1_w_gu": n(ks[7], (DEPTH, D_MODEL, 2 * D_FF), D_MODEL ** -0.5),
        "ffn1_w_down": n(ks[8], (DEPTH, D_FF, D_MODEL), D_FF ** -0.5),
        "ffn2_w_gu": n(ks[9], (DEPTH, D_MODEL, 2 * D_FF), D_MODEL ** -0.5),
        "ffn2_w_down": n(ks[10], (DEPTH, D_FF, D_MODEL), D_FF ** -0.5),
        "w_in": n(ks[11], (DEPTH, D_MODEL, IN_COLS), D_MODEL ** -0.5),
        "w_out": n(ks[12], (DEPTH, MIX_WIDTH, D_MODEL), MIX_WIDTH ** -0.5),
        "q_norm_w": 1.0 + n(ks[13], (DEPTH, ATTN_HEAD_DIM), 0.1),
        "k_norm_w": 1.0 + n(ks[14], (DEPTH, ATTN_HEAD_DIM), 0.1),
        "lambda_qk": n(ks[15], (DEPTH, 4, ATTN_HEAD_DIM), 0.1),
        "subln_w": 1.0 + n(ks[16], (DEPTH, ATTN_V_DIM), 0.1),
        "conv_w": n(ks[17], (DEPTH, CONV_K, CONV_WIDTH), CONV_K ** -0.5),
        "pool_w": n(ks[18], (DEPTH, POOL_GROUPS, POOL_GROUP_DIM, POOL_GROUP_DIM), POOL_GROUP_DIM ** -0.5),
        "pool_scale": 1.0 + n(ks[19], (DEPTH, POOL_WIDTH), 0.1),
    }


def reference(x, c, ctx, c_ctx, norm_w, w_mod, b_mod, ffn1_w_gu, ffn1_w_down,
              ffn2_w_gu, ffn2_w_down, w_in, w_out, q_norm_w, k_norm_w, lambda_qk,
              subln_w, conv_w, pool_w, pool_scale):
    B, S, _ = x.shape
    cos, sin = _axial_rope_tables(S)
    silu_c = jax.nn.silu(c)
    silu_cc = jax.nn.silu(c_ctx)[None]
    h, hc = x, ctx
    for l in range(DEPTH):
        ctx_out = l < DEPTH - 1
        lam_init = 0.8 - 0.6 * math.exp(-0.3 * l)
        mod = (silu_c @ w_mod[l] + b_mod[l]).reshape(B, N_MOD, D_MODEL)
        mod_c = (silu_cc @ w_mod[l] + b_mod[l]).reshape(1, N_MOD, D_MODEL)
        m = [mod[:, i] for i in range(N_MOD)]
        mc = [mod_c[:, i] for i in range(N_MOD)]
        h = h + 0.5 * m[2][:, None] * _swiglu(_modulate(_rms(h, norm_w[l, 0]), m[0], m[1]),
                                              ffn1_w_gu[l], ffn1_w_down[l])
        hc = hc + 0.5 * mc[2][:, None] * _swiglu(_modulate(_rms(hc, norm_w[l, 0]), mc[0], mc[1]),
                                                 ffn1_w_gu[l], ffn1_w_down[l])
        lq = lambda_qk[l].astype(jnp.float32)
        lam = jnp.exp(jnp.sum(lq[0] * lq[1])) - jnp.exp(jnp.sum(lq[2] * lq[3])) + lam_init
        u = _modulate(_rms(h, norm_w[l, 1]), m[3], m[4])
        uc = _modulate(_rms(hc, norm_w[l, 1]), mc[3], mc[4])
        y, yc = _hybrid_mixer(u, uc, ctx_out, w_in[l], w_out[l], q_norm_w[l], k_norm_w[l],
                              lam, lam_init, subln_w[l], conv_w[l], pool_w[l], pool_scale[l],
                              cos, sin)
        h = h + m[5][:, None] * y
        h = h + 0.5 * m[8][:, None] * _swiglu(_modulate(_rms(h, norm_w[l, 2]), m[6], m[7]),
                                              ffn2_w_gu[l], ffn2_w_down[l])
        if ctx_out:
            hc = hc + mc[5][:, None] * yc
            hc = hc + 0.5 * mc[8][:, None] * _swiglu(_modulate(_rms(hc, norm_w[l, 2]), mc[6], mc[7]),
                                                     ffn2_w_gu[l], ffn2_w_down[l])
    return h
```

```python
import functools
import math

import jax
import jax.numpy as jnp
from jax import lax
from jax.experimental import pallas as pl
from jax.experimental.pallas import tpu as pltpu

F32 = jnp.float32
BF16 = jnp.bfloat16

GRID_W = 64
ATTN_HEADS = 4
ATTN_HEAD_DIM = 64
ATTN_V_DIM = 2 * ATTN_HEAD_DIM
ATTN_WIDTH = ATTN_HEADS * ATTN_V_DIM
POOL_WINDOWS = (2, 4, 8, 16)
ROPE_BASE = 10000.0
N_MOD = 9
EPS = 1e-6

LANES = 128
SUBLANES = 8
HALO = SUBLANES
MXU_DIM = 256
VMEM_LIMIT = 56 * 1024 * 1024


def _const_spec(shape):
    zeros = (0,) * len(shape)
    return pl.BlockSpec(shape, lambda *_: zeros, pipeline_mode=pl.Buffered(1))


def _params(n_axes):
    return pltpu.CompilerParams(dimension_semantics=("parallel",) * n_axes,
                                vmem_limit_bytes=VMEM_LIMIT)


def _rms_mod(x, nw, shift, scale):
    ms = jnp.mean(x * x, axis=-1, keepdims=True)
    y = x * lax.rsqrt(ms + EPS) * nw
    return y * (1.0 + scale) + shift


def _silu(x):
    return x * jax.nn.sigmoid(x)


def _mods_kernel(c_ref, w_ref, b_ref, o_ref):
    s = _silu(c_ref[...]).astype(BF16)
    o_ref[...] = jnp.dot(s, w_ref[...].astype(BF16), preferred_element_type=F32) + b_ref[...]


def _mods_call(cc, w_mod, b_mod):
    depth, d, n = w_mod.shape
    rows = cc.shape[0]
    tn = d
    return pl.pallas_call(
        _mods_kernel,
        grid=(depth, n // tn),
        in_specs=[
            pl.BlockSpec((rows, d), lambda l, j: (0, 0)),
            pl.BlockSpec((None, d, tn), lambda l, j: (l, 0, j)),
            pl.BlockSpec((None, 1, tn), lambda l, j: (l, 0, j)),
        ],
        out_specs=pl.BlockSpec((None, rows, tn), lambda l, j: (l, 0, j)),
        out_shape=jax.ShapeDtypeStruct((depth, rows, n), F32),
        compiler_params=_params(2),
        name="mods",
    )(cc, w_mod, b_mod.reshape(depth, 1, n))


def _ffn_kernel(h_ref, mod_ref, nw_ref, wgu_ref, wd_ref, o_ref, *, i0):
    x = h_ref[...]
    d_ff = wd_ref.shape[0]
    xn = _rms_mod(x, nw_ref[...], mod_ref[i0:i0 + 1, :], mod_ref[i0 + 1:i0 + 2, :]).astype(BF16)
    gu = jnp.dot(xn, wgu_ref[...], preferred_element_type=F32)
    a = (_silu(gu[:, :d_ff]) * gu[:, d_ff:]).astype(BF16)
    f = jnp.dot(a, wd_ref[...], preferred_element_type=F32)
    o_ref[...] = x + (0.5 * mod_ref[i0 + 2:i0 + 3, :]) * f


def _ffn_call(h, mods, mod_row, nw, wgu, wd, *, i0, tm):
    rows, d = h.shape
    d_ff = wd.shape[0]
    return pl.pallas_call(
        functools.partial(_ffn_kernel, i0=i0),
        grid=(rows // tm,),
        in_specs=[
            pl.BlockSpec((tm, d), lambda j: (j, 0)),
            pl.BlockSpec((None, N_MOD, d), lambda j: (mod_row(j), 0, 0)),
            _const_spec((1, d)),
            _const_spec((d, 2 * d_ff)),
            _const_spec((d_ff, d)),
        ],
        out_specs=pl.BlockSpec((tm, d), lambda j: (j, 0)),
        out_shape=jax.ShapeDtypeStruct((rows, d), F32),
        compiler_params=_params(1),
        name="ffn",
    )(h, mods, nw, wgu, wd)


def _swap_halves(z):
    lane = lax.broadcasted_iota(jnp.int32, z.shape, 1)
    up = pltpu.roll(z, LANES - 16, axis=1)
    down = pltpu.roll(z, 16, axis=1)
    return jnp.where((lane & 31) < 16, up, down)


def _head_norm(z, w_ref, g_ref):
    outs = []
    for s in range(z.shape[1] // MXU_DIM):
        zs = z[:, s * MXU_DIM:(s + 1) * MXU_DIM]
        ss = jnp.dot((zs * zs).astype(BF16), g_ref[...], preferred_element_type=F32)
        outs.append(zs * lax.rsqrt(ss * (1.0 / ATTN_HEAD_DIM) + EPS)
                    * w_ref[:, s * MXU_DIM:(s + 1) * MXU_DIM])
    return outs


def _rope(z, cos, sin):
    outs = []
    for s in range(z.shape[1] // LANES):
        zs = z[:, s * LANES:(s + 1) * LANES]
        outs.append(zs * cos + _swap_halves(zs) * sin)
    return jnp.concatenate(outs, axis=1)


def _proj_kernel(h_ref, mod_ref, nw_ref, win_ref, g_ref, qw_ref, kw_ref, *rest, rope):
    if rope:
        cos_ref, sin_ref, q_ref, k_ref, v_ref, cp_ref = rest
    else:
        q_ref, k_ref, v_ref, cp_ref = rest
    aw = ATTN_WIDTH
    cw = (win_ref.shape[1] - 3 * aw) // 4
    u = _rms_mod(h_ref[...], nw_ref[...], mod_ref[3:4, :], mod_ref[4:5, :]).astype(BF16)
    p = jnp.dot(u, win_ref[...], preferred_element_type=F32)
    for z, w_ref, o_ref, sc in ((p[:, :aw], qw_ref, q_ref, ATTN_HEAD_DIM ** -0.5),
                                (p[:, aw:2 * aw], kw_ref, k_ref, None)):
        slabs = _head_norm(z, w_ref, g_ref)
        if rope:
            slabs = [_rope(zs, cos_ref[...], sin_ref[...]) for zs in slabs]
        for s, zs in enumerate(slabs):
            if sc is not None:
                zs = zs * sc
            o_ref[:, s * MXU_DIM:(s + 1) * MXU_DIM] = zs.astype(BF16)
    v_ref[...] = p[:, 2 * aw:3 * aw].astype(BF16)
    c0 = 3 * aw
    cp_ref[:, :cw] = p[:, c0:c0 + cw]
    cp_ref[:, cw:2 * cw] = p[:, c0 + cw:c0 + 2 * cw] * p[:, c0 + 2 * cw:c0 + 3 * cw]
    cp_ref[:, 2 * cw:] = p[:, c0 + 3 * cw:]


def _proj_call(h, mods, mod_row, nw, win, gmat, qw, kw, cos, sin, *, tm, tiles_per_seq):
    rows, d = h.shape
    ncol = win.shape[1]
    aw = ATTN_WIDTH
    cw = (ncol - 3 * aw) // 4
    rope = cos is not None
    in_specs = [
        pl.BlockSpec((tm, d), lambda j: (j, 0)),
        pl.BlockSpec((None, N_MOD, d), lambda j: (mod_row(j), 0, 0)),
        _const_spec((1, d)),
        _const_spec((d, ncol)),
        _const_spec((MXU_DIM, MXU_DIM)),
        _const_spec((1, aw)),
        _const_spec((1, aw)),
    ]
    args = [h, mods, nw, win, gmat, qw, kw]
    if rope:
        in_specs += [pl.BlockSpec((tm, LANES), lambda j: (j % tiles_per_seq, 0))] * 2
        args += [cos, sin]
    return pl.pallas_call(
        functools.partial(_proj_kernel, rope=rope),
        grid=(rows // tm,),
        in_specs=in_specs,
        out_specs=[pl.BlockSpec((tm, aw), lambda j: (j, 0))] * 3
        + [pl.BlockSpec((tm, 3 * cw), lambda j: (j, 0))],
        out_shape=[jax.ShapeDtypeStruct((rows, aw), BF16)] * 3
        + [jax.ShapeDtypeStruct((rows, 3 * cw), F32)],
        compiler_params=_params(1),
        name="proj",
    )(*args)


def _attn_kernel(lq_ref, sw_ref, q_ref, *rest, n_kv, lam_init):
    kv_refs, o_ref = rest[:-1], rest[-1]
    k_refs, v_refs = kv_refs[:n_kv], kv_refs[n_kv:]
    lq = lq_ref[...]
    lam = (jnp.exp(jnp.sum(lq[0:1] * lq[1:2], axis=-1, keepdims=True))
           - jnp.exp(jnp.sum(lq[2:3] * lq[3:4], axis=-1, keepdims=True)) + lam_init)
    q = q_ref[...]
    lane = lax.broadcasted_iota(jnp.int32, q.shape, 1)
    nt = (((1,), (1,)), ((), ()))

    def softmax_parts(qm):
        s = [lax.dot_general(qm, k[...], nt, preferred_element_type=F32) for k in k_refs]
        m = functools.reduce(jnp.maximum, [jnp.max(x, axis=-1, keepdims=True) for x in s])
        e = [jnp.exp(x - m) for x in s]
        l = functools.reduce(jnp.add, [jnp.sum(x, axis=-1, keepdims=True) for x in e])
        return e, 1.0 / l

    e0, r0 = softmax_parts(jnp.where(lane < ATTN_HEAD_DIM, q, jnp.zeros_like(q)))
    e1, r1 = softmax_parts(jnp.where(lane >= ATTN_HEAD_DIM, q, jnp.zeros_like(q)))
    r1 = lam * r1
    o = None
    for a0, a1, v in zip(e0, e1, v_refs):
        a = (a0 * r0 - a1 * r1).astype(BF16)
        pv = jnp.dot(a, v[...], preferred_element_type=F32)
        o = pv if o is None else o + pv
    ms = jnp.mean(o * o, axis=-1, keepdims=True)
    o_ref[...] = (o * lax.rsqrt(ms + EPS) * sw_ref[...] * (1.0 - lam_init)).astype(BF16)


def _attn_call(lq, sw, q, ks, vs, *, batch, tq, lam_init):
    rows = q.shape[0]
    nq = rows // batch // tq
    hd = ATTN_V_DIM
    kv_specs = [pl.BlockSpec((k.shape[0] // batch, hd), lambda b, h, i: (b, h)) for k in ks + vs]
    return pl.pallas_call(
        functools.partial(_attn_kernel, n_kv=len(ks), lam_init=lam_init),
        grid=(batch, ATTN_HEADS, nq),
        in_specs=[
            _const_spec(lq.shape),
            _const_spec(sw.shape),
            pl.BlockSpec((tq, hd), lambda b, h, i: (b * nq + i, h)),
        ] + kv_specs,
        out_specs=pl.BlockSpec((tq, hd), lambda b, h, i: (b * nq + i, h)),
        out_shape=jax.ShapeDtypeStruct((rows, ATTN_WIDTH), BF16),
        compiler_params=_params(3),
        name="attn",
    )(lq, sw, q, *ks, *vs)


def _shift_rows(a, d):
    n = a.shape[0]
    return pltpu.roll(a, (-d) % n, axis=0)


def _mix_kernel(h_ref, mod_ref, attn_ref, cp_ref, prev_ref, next_ref, cw_ref, pw_ref, ps_ref,
                wout_ref, o_ref, *, tiles_per_seq, seq_len):
    tm = h_ref.shape[0]
    cw = cp_ref.shape[1] // 3
    jj = pl.program_id(0) % tiles_per_seq
    keep_prev = (jj > 0).astype(F32)
    keep_next = (jj < tiles_per_seq - 1).astype(F32)
    ext = jnp.concatenate([prev_ref[:, cw:] * keep_prev, cp_ref[:, cw:], next_ref[:, cw:] * keep_next],
                          axis=0)
    cur = slice(HALO, HALO + tm)
    uc = ext[:, :cw]
    conv = cp_ref[:, :cw] * (_shift_rows(uc, -1)[cur] * cw_ref[0:1, :] + uc[cur] * cw_ref[1:2, :]
                             + _shift_rows(uc, 1)[cur] * cw_ref[2:3, :])
    t = jj * tm + lax.broadcasted_iota(jnp.int32, (tm, 1), 0)
    cnt = [(jnp.minimum(t + w // 2, seq_len) - jnp.maximum(t - w // 2, 0)).astype(F32)
           for w in POOL_WINDOWS]
    lane = lax.broadcasted_iota(jnp.int32, (tm, LANES), 1)
    pooled = []
    for s in range(cw // LANES):
        x = ext[:, cw + s * LANES:cw + (s + 1) * LANES]
        s2 = x + _shift_rows(x, -1)
        s4 = _shift_rows(s2, -1) + _shift_rows(s2, 1)
        if s == 0:
            lo, hi, c_lo, c_hi = s2, s4, cnt[0], cnt[1]
        else:
            s8 = _shift_rows(s4, -2) + _shift_rows(s4, 2)
            s16 = _shift_rows(s8, -4) + _shift_rows(s8, 4)
            lo, hi, c_lo, c_hi = s8, s16, cnt[2], cnt[3]
        pooled.append(jnp.where(lane < 64, lo[cur] / c_lo, hi[cur] / c_hi) - x[cur])
    pooled = jnp.concatenate(pooled, axis=1).astype(BF16)
    pool = jnp.dot(pooled, pw_ref[...], preferred_element_type=F32) * ps_ref[...]
    aw = attn_ref.shape[1]
    y = (jnp.dot(attn_ref[...], wout_ref[:aw, :], preferred_element_type=F32)
         + jnp.dot(conv.astype(BF16), wout_ref[aw:aw + cw, :], preferred_element_type=F32)
         + jnp.dot(pool.astype(BF16), wout_ref[aw + cw:, :], preferred_element_type=F32))
    o_ref[...] = h_ref[...] + mod_ref[5:6, :] * y


def _mix_call(h, mods, mod_row, attn, cp, conv_w, pool_w, pool_s, wout, *, tm, tiles_per_seq):
    rows, d = h.shape
    aw = attn.shape[1]
    cw3 = cp.shape[1]
    hpt = tm // HALO
    last = rows // HALO - 1
    return pl.pallas_call(
        functools.partial(_mix_kernel, tiles_per_seq=tiles_per_seq, seq_len=tm * tiles_per_seq),
        grid=(rows // tm,),
        in_specs=[
            pl.BlockSpec((tm, d), lambda j: (j, 0)),
            pl.BlockSpec((None, N_MOD, d), lambda j: (mod_row(j), 0, 0)),
            pl.BlockSpec((tm, aw), lambda j: (j, 0)),
            pl.BlockSpec((tm, cw3), lambda j: (j, 0)),
            pl.BlockSpec((HALO, cw3), lambda j: (jnp.maximum(j * hpt - 1, 0), 0)),
            pl.BlockSpec((HALO, cw3), lambda j: (jnp.minimum((j + 1) * hpt, last), 0)),
            _const_spec(conv_w.shape),
            _const_spec(pool_w.shape),
            _const_spec(pool_s.shape),
            _const_spec(wout.shape),
        ],
        out_specs=pl.BlockSpec((tm, d), lambda j: (j, 0)),
        out_shape=jax.ShapeDtypeStruct((rows, d), F32),
        compiler_params=_params(1),
        name="mix",
    )(h, mods, attn, cp, cp, cp, conv_w, pool_w, pool_s, wout)


def _rope_tables(n_tokens):
    pos = jnp.arange(n_tokens)
    row = (pos // GRID_W).astype(F32)
    col = (pos % GRID_W).astype(F32)
    nfreq = ATTN_HEAD_DIM // 4
    inv = ROPE_BASE ** (-jnp.arange(nfreq, dtype=F32) / nfreq)
    ar, ac = row[:, None] * inv, col[:, None] * inv
    cos = jnp.concatenate([jnp.cos(ar), jnp.cos(ar), jnp.cos(ac), jnp.cos(ac)], axis=1)
    sin = jnp.concatenate([-jnp.sin(ar), jnp.sin(ar), -jnp.sin(ac), jnp.sin(ac)], axis=1)
    return jnp.tile(cos, (1, 2)), jnp.tile(sin, (1, 2))


def _block_diag(blocks):
    g, n, _ = blocks.shape
    eye = jnp.eye(g, dtype=blocks.dtype)
    return (eye[:, None, :, None] * blocks[:, :, None, :]).reshape(g * n, g * n)


def kernel(x, c, ctx, c_ctx, norm_w, w_mod, b_mod, ffn1_w_gu, ffn1_w_down, ffn2_w_gu, ffn2_w_down,
           w_in, w_out, q_norm_w, k_norm_w, lambda_qk, subln_w, conv_w, pool_w, pool_scale):
    batch, seq, d = x.shape
    ctx_len = ctx.shape[1]
    depth = w_mod.shape[0]
    n_heads_maps = ATTN_WIDTH // ATTN_HEAD_DIM

    cc = jnp.concatenate([c, c_ctx[None]], axis=0)
    mods_all = _mods_call(cc, w_mod, b_mod).reshape(depth, batch + 1, N_MOD, d)

    wgu1, wd1 = ffn1_w_gu.astype(BF16), ffn1_w_down.astype(BF16)
    wgu2, wd2 = ffn2_w_gu.astype(BF16), ffn2_w_down.astype(BF16)
    win, wout = w_in.astype(BF16), w_out.astype(BF16)
    gmat = _block_diag(jnp.ones((MXU_DIM // ATTN_HEAD_DIM, ATTN_HEAD_DIM, ATTN_HEAD_DIM), BF16))
    cos, sin = _rope_tables(seq)

    tm_lat, tm_ctx = 512, 256
    lat_row = lambda j: j // (seq // tm_lat)
    ctx_row = lambda j: batch

    h = x.reshape(batch * seq, d)
    hc = ctx.reshape(batch * ctx_len, d)
    for l in range(depth):
        ctx_out = l < depth - 1
        lam_init = 0.8 - 0.6 * math.exp(-0.3 * l)
        mods = mods_all[l]
        nw = norm_w[l][:, None, :]
        qw = jnp.tile(q_norm_w[l], n_heads_maps)[None]
        kw = jnp.tile(k_norm_w[l], n_heads_maps)[None]
        pw = _block_diag(pool_w[l]).astype(BF16)
        ps = pool_scale[l][None]
        sw = subln_w[l][None]

        h = _ffn_call(h, mods, lat_row, nw[0], wgu1[l], wd1[l], i0=0, tm=tm_lat)
        hc = _ffn_call(hc, mods, ctx_row, nw[0], wgu1[l], wd1[l], i0=0, tm=tm_lat)

        q, k, v, cp = _proj_call(h, mods, lat_row, nw[1], win[l], gmat, qw, kw, cos, sin,
                                 tm=tm_lat, tiles_per_seq=seq // tm_lat)
        qc, kc, vc, cpc = _proj_call(hc, mods, ctx_row, nw[1], win[l], gmat, qw, kw, None, None,
                                     tm=tm_ctx, tiles_per_seq=1)

        attn = _attn_call(lambda_qk[l], sw, q, [k, kc], [v, vc], batch=batch, tq=256, lam_init=lam_init)
        h = _mix_call(h, mods, lat_row, attn, cp, conv_w[l], pw, ps, wout[l],
                      tm=tm_lat, tiles_per_seq=seq // tm_lat)
        h = _ffn_call(h, mods, lat_row, nw[2], wgu2[l], wd2[l], i0=6, tm=tm_lat)
        if ctx_out:
            attn_c = _attn_call(lambda_qk[l], sw, qc, [kc], [vc], batch=batch, tq=ctx_len,
                                lam_init=lam_init)
            hc = _mix_call(hc, mods, ctx_row, attn_c, cpc, conv_w[l], pw, ps, wout[l],
                           tm=tm_ctx, tiles_per_seq=1)
            hc = _ffn_call(hc, mods, ctx_row, nw[2], wgu2[l], wd2[l], i0=6, tm=tm_lat)
    return h.reshape(batch, seq, d)
```

```python
import functools
import math

import jax
import jax.numpy as jnp
from jax import lax
from jax.experimental import pallas as pl
from jax.experimental.pallas import tpu as pltpu

F32 = jnp.float32
BF16 = jnp.bfloat16

GRID_W = 64
ATTN_HEADS = 4
ATTN_HEAD_DIM = 64
ATTN_V_DIM = 2 * ATTN_HEAD_DIM
ATTN_WIDTH = ATTN_HEADS * ATTN_V_DIM
POOL_WINDOWS = (2, 4, 8, 16)
ROPE_BASE = 10000.0
N_MOD = 9
EPS = 1e-6
Q_SCALE = ATTN_HEAD_DIM ** -0.5 * math.log2(math.e)

LANES = 128
SUBLANES = 8
HALO = SUBLANES
MXU_DIM = 256
VMEM_LIMIT = 56 * 1024 * 1024


def _const_spec(shape):
    zeros = (0,) * len(shape)
    return pl.BlockSpec(shape, lambda *_: zeros, pipeline_mode=pl.Buffered(1))


def _layer_spec(shape, l):
    idx = (l,) + (0,) * len(shape)
    return pl.BlockSpec((None,) + tuple(shape), lambda *_: idx, pipeline_mode=pl.Buffered(1))


def _params(n_axes):
    return pltpu.CompilerParams(dimension_semantics=("parallel",) * n_axes,
                                vmem_limit_bytes=VMEM_LIMIT)


def _rms_mod(x, nw, shift, scale):
    ms = jnp.mean(x * x, axis=-1, keepdims=True)
    y = x * lax.rsqrt(ms + EPS) * nw
    return y * (1.0 + scale) + shift


def _silu(x):
    return x * jax.nn.sigmoid(x)


def _mods_kernel(c_ref, w_ref, b_ref, o_ref):
    s = _silu(c_ref[...]).astype(BF16)
    o_ref[...] = jnp.dot(s, w_ref[...].astype(BF16), preferred_element_type=F32) + b_ref[...]


def _mods_call(cc, w_mod, b_mod):
    depth, d, n = w_mod.shape
    rows = cc.shape[0]
    tn = d
    return pl.pallas_call(
        _mods_kernel,
        grid=(depth, n // tn),
        in_specs=[
            pl.BlockSpec((rows, d), lambda l, j: (0, 0)),
            pl.BlockSpec((None, d, tn), lambda l, j: (l, 0, j)),
            pl.BlockSpec((None, 1, tn), lambda l, j: (l, 0, j)),
        ],
        out_specs=pl.BlockSpec((None, rows, tn), lambda l, j: (l, 0, j)),
        out_shape=jax.ShapeDtypeStruct((depth, rows, n), F32),
        compiler_params=_params(2),
        name="mods",
    )(cc, w_mod, b_mod.reshape(depth, 1, n))


def _ffn_kernel(h_ref, mod_ref, nw_ref, wgu_ref, wd_ref, o_ref, *, i0):
    x = h_ref[...]
    d_ff = wd_ref.shape[0]
    xn = _rms_mod(x, nw_ref[...], mod_ref[i0:i0 + 1, :], mod_ref[i0 + 1:i0 + 2, :]).astype(BF16)
    gu = jnp.dot(xn, wgu_ref[...], preferred_element_type=F32)
    a = (_silu(gu[:, :d_ff]) * gu[:, d_ff:]).astype(BF16)
    f = jnp.dot(a, wd_ref[...], preferred_element_type=F32)
    o_ref[...] = x + (0.5 * mod_ref[i0 + 2:i0 + 3, :]) * f


def _ffn_call(h, mods, mod_row, nw, wgu, wd, *, l, i0, tm):
    rows, d = h.shape
    d_ff = wd.shape[1]
    return pl.pallas_call(
        functools.partial(_ffn_kernel, i0=i0),
        grid=(rows // tm,),
        in_specs=[
            pl.BlockSpec((tm, d), lambda j: (j, 0)),
            pl.BlockSpec((None, N_MOD, d), lambda j: (mod_row(j), 0, 0)),
            _const_spec((1, d)),
            _layer_spec((d, 2 * d_ff), l),
            _layer_spec((d_ff, d), l),
        ],
        out_specs=pl.BlockSpec((tm, d), lambda j: (j, 0)),
        out_shape=jax.ShapeDtypeStruct((rows, d), F32),
        compiler_params=_params(1),
        name="ffn",
    )(h, mods, nw, wgu, wd)


def _swap_halves(z):
    lane = lax.broadcasted_iota(jnp.int32, z.shape, 1)
    up = pltpu.roll(z, LANES - 16, axis=1)
    down = pltpu.roll(z, 16, axis=1)
    return jnp.where((lane & 31) < 16, up, down)


def _head_norm(z, w_ref, g_ref):
    outs = []
    for s in range(z.shape[1] // MXU_DIM):
        zs = z[:, s * MXU_DIM:(s + 1) * MXU_DIM]
        ss = jnp.dot((zs * zs).astype(BF16), g_ref[...], preferred_element_type=F32)
        outs.append(zs * lax.rsqrt(ss * (1.0 / ATTN_HEAD_DIM) + EPS)
                    * w_ref[:, s * MXU_DIM:(s + 1) * MXU_DIM])
    return outs


def _rope(z, cos, sin):
    outs = []
    for s in range(z.shape[1] // LANES):
        zs = z[:, s * LANES:(s + 1) * LANES]
        outs.append(zs * cos + _swap_halves(zs) * sin)
    return jnp.concatenate(outs, axis=1)


def _proj_kernel(h_ref, mod_ref, nw_ref, win_ref, g_ref, qw_ref, kw_ref, *rest, rope):
    if rope:
        cos_ref, sin_ref, q_ref, k_ref, v_ref, cp_ref = rest
    else:
        q_ref, k_ref, v_ref, cp_ref = rest
    aw = ATTN_WIDTH
    cw = (win_ref.shape[1] - 3 * aw) // 4
    u = _rms_mod(h_ref[...], nw_ref[...], mod_ref[3:4, :], mod_ref[4:5, :]).astype(BF16)
    p = jnp.dot(u, win_ref[...], preferred_element_type=F32)
    for z, w_ref, o_ref, sc in ((p[:, :aw], qw_ref, q_ref, Q_SCALE),
                                (p[:, aw:2 * aw], kw_ref, k_ref, None)):
        slabs = _head_norm(z, w_ref, g_ref)
        if rope:
            slabs = [_rope(zs, cos_ref[...], sin_ref[...]) for zs in slabs]
        for s, zs in enumerate(slabs):
            if sc is not None:
                zs = zs * sc
            o_ref[:, s * MXU_DIM:(s + 1) * MXU_DIM] = zs.astype(BF16)
    v_ref[...] = p[:, 2 * aw:3 * aw].astype(BF16)
    c0 = 3 * aw
    cp_ref[:, :cw] = p[:, c0:c0 + cw]
    cp_ref[:, cw:2 * cw] = p[:, c0 + cw:c0 + 2 * cw] * p[:, c0 + 2 * cw:c0 + 3 * cw]
    cp_ref[:, 2 * cw:] = p[:, c0 + 3 * cw:]


def _proj_call(h, mods, mod_row, nw, win, gmat, qw, kw, cos, sin, *, l, tm, tiles_per_seq):
    rows, d = h.shape
    ncol = win.shape[2]
    aw = ATTN_WIDTH
    cw = (ncol - 3 * aw) // 4
    rope = cos is not None
    in_specs = [
        pl.BlockSpec((tm, d), lambda j: (j, 0)),
        pl.BlockSpec((None, N_MOD, d), lambda j: (mod_row(j), 0, 0)),
        _const_spec((1, d)),
        _layer_spec((d, ncol), l),
        _const_spec((MXU_DIM, MXU_DIM)),
        _const_spec((1, aw)),
        _const_spec((1, aw)),
    ]
    args = [h, mods, nw, win, gmat, qw, kw]
    if rope:
        in_specs += [pl.BlockSpec((tm, LANES), lambda j: (j % tiles_per_seq, 0))] * 2
        args += [cos, sin]
    return pl.pallas_call(
        functools.partial(_proj_kernel, rope=rope),
        grid=(rows // tm,),
        in_specs=in_specs,
        out_specs=[pl.BlockSpec((tm, aw), lambda j: (j, 0))] * 3
        + [pl.BlockSpec((tm, 3 * cw), lambda j: (j, 0))],
        out_shape=[jax.ShapeDtypeStruct((rows, aw), BF16)] * 3
        + [jax.ShapeDtypeStruct((rows, 3 * cw), F32)],
        compiler_params=_params(1),
        name="proj",
    )(*args)


def _attn_kernel(lq_ref, sw_ref, q_ref, *rest, n_kv, tq, lam_init):
    kv_refs, o_ref = rest[:-1], rest[-1]
    k_refs, v_refs = kv_refs[:n_kv], kv_refs[n_kv:]
    lq = lq_ref[...]
    lam = (jnp.exp(jnp.sum(lq[0:1] * lq[1:2], axis=-1, keepdims=True))
           - jnp.exp(jnp.sum(lq[2:3] * lq[3:4], axis=-1, keepdims=True)) + lam_init)
    lane = lax.broadcasted_iota(jnp.int32, (tq, ATTN_V_DIM), 1)
    nt = (((1,), (1,)), ((), ()))
    va = [jnp.concatenate([v[...], jnp.ones(v.shape, BF16)], axis=1) for v in v_refs]

    def probs(i):
        q = q_ref[i * tq:(i + 1) * tq, :]
        q2 = jnp.concatenate([jnp.where(lane < ATTN_HEAD_DIM, q, jnp.zeros_like(q)),
                              jnp.where(lane >= ATTN_HEAD_DIM, q, jnp.zeros_like(q))], axis=0)
        s = [lax.dot_general(q2, k[...], nt, preferred_element_type=F32) for k in k_refs]
        m = functools.reduce(jnp.maximum, [jnp.max(x, axis=-1, keepdims=True) for x in s])
        return [jnp.exp2(x - m).astype(BF16) for x in s]

    def attend(i, e):
        acc = None
        for x, v in zip(e, va):
            pv = jnp.dot(x, v, preferred_element_type=F32)
            acc = pv if acc is None else acc + pv
        av = acc[:, :ATTN_V_DIM] / acc[:, ATTN_V_DIM:]
        o = av[:tq] - lam * av[tq:]
        ms = jnp.mean(o * o, axis=-1, keepdims=True)
        o_ref[i * tq:(i + 1) * tq, :] = (o * lax.rsqrt(ms + EPS) * sw_ref[...]
                                         * (1.0 - lam_init)).astype(BF16)

    nq = q_ref.shape[0] // tq
    e = probs(0)
    for i in range(nq):
        e_next = probs(i + 1) if i + 1 < nq else None
        attend(i, e)
        e = e_next


def _attn_call(lq, sw, q, ks, vs, *, batch, tq, lam_init):
    rows = q.shape[0]
    lq_len = rows // batch
    hd = ATTN_V_DIM
    kv_specs = [pl.BlockSpec((k.shape[0] // batch, hd), lambda b, h: (b, h)) for k in ks + vs]
    return pl.pallas_call(
        functools.partial(_attn_kernel, n_kv=len(ks), tq=tq, lam_init=lam_init),
        grid=(batch, ATTN_HEADS),
        in_specs=[
            _const_spec(lq.shape),
            _const_spec(sw.shape),
            pl.BlockSpec((lq_len, hd), lambda b, h: (b, h)),
        ] + kv_specs,
        out_specs=pl.BlockSpec((lq_len, hd), lambda b, h: (b, h)),
        out_shape=jax.ShapeDtypeStruct((rows, ATTN_WIDTH), BF16),
        compiler_params=_params(2),
        name="attn",
    )(lq, sw, q, *ks, *vs)


def _shift_rows(a, d):
    n = a.shape[0]
    return pltpu.roll(a, (-d) % n, axis=0)


def _mix_kernel(h_ref, mod_ref, attn_ref, cp_ref, prev_ref, next_ref, cw_ref, pw_ref, ps_ref,
                wout_ref, o_ref, *, tiles_per_seq, seq_len):
    tm = h_ref.shape[0]
    cw = cp_ref.shape[1] // 3
    jj = pl.program_id(0) % tiles_per_seq
    keep_prev = (jj > 0).astype(F32)
    keep_next = (jj < tiles_per_seq - 1).astype(F32)
    ext = jnp.concatenate([prev_ref[:, cw:] * keep_prev, cp_ref[:, cw:], next_ref[:, cw:] * keep_next],
                          axis=0)
    cur = slice(HALO, HALO + tm)
    uc = ext[:, :cw]
    conv = cp_ref[:, :cw] * (_shift_rows(uc, -1)[cur] * cw_ref[0:1, :] + uc[cur] * cw_ref[1:2, :]
                             + _shift_rows(uc, 1)[cur] * cw_ref[2:3, :])
    t = jj * tm + lax.broadcasted_iota(jnp.int32, (tm, 1), 0)
    cnt = [(jnp.minimum(t + w // 2, seq_len) - jnp.maximum(t - w // 2, 0)).astype(F32)
           for w in POOL_WINDOWS]
    lane = lax.broadcasted_iota(jnp.int32, (tm, LANES), 1)
    pooled = []
    for s in range(cw // LANES):
        x = ext[:, cw + s * LANES:cw + (s + 1) * LANES]
        s2 = x + _shift_rows(x, -1)
        s4 = _shift_rows(s2, -1) + _shift_rows(s2, 1)
        if s == 0:
            lo, hi, c_lo, c_hi = s2, s4, cnt[0], cnt[1]
        else:
            s8 = _shift_rows(s4, -2) + _shift_rows(s4, 2)
            s16 = _shift_rows(s8, -4) + _shift_rows(s8, 4)
            lo, hi, c_lo, c_hi = s8, s16, cnt[2], cnt[3]
        pooled.append(jnp.where(lane < 64, lo[cur] / c_lo, hi[cur] / c_hi) - x[cur])
    pooled = jnp.concatenate(pooled, axis=1).astype(BF16)
    pool = jnp.dot(pooled, pw_ref[...], preferred_element_type=F32) * ps_ref[...]
    aw = attn_ref.shape[1]
    y = (jnp.dot(attn_ref[...], wout_ref[:aw, :], preferred_element_type=F32)
         + jnp.dot(conv.astype(BF16), wout_ref[aw:aw + cw, :], preferred_element_type=F32)
         + jnp.dot(pool.astype(BF16), wout_ref[aw + cw:, :], preferred_element_type=F32))
    o_ref[...] = h_ref[...] + mod_ref[5:6, :] * y


def _mix_call(h, mods, mod_row, attn, cp, conv_w, pool_w, pool_s, wout, *, l, tm, tiles_per_seq):
    rows, d = h.shape
    aw = attn.shape[1]
    cw3 = cp.shape[1]
    hpt = tm // HALO
    last = rows // HALO - 1
    return pl.pallas_call(
        functools.partial(_mix_kernel, tiles_per_seq=tiles_per_seq, seq_len=tm * tiles_per_seq),
        grid=(rows // tm,),
        in_specs=[
            pl.BlockSpec((tm, d), lambda j: (j, 0)),
            pl.BlockSpec((None, N_MOD, d), lambda j: (mod_row(j), 0, 0)),
            pl.BlockSpec((tm, aw), lambda j: (j, 0)),
            pl.BlockSpec((tm, cw3), lambda j: (j, 0)),
            pl.BlockSpec((HALO, cw3), lambda j: (jnp.maximum(j * hpt - 1, 0), 0)),
            pl.BlockSpec((HALO, cw3), lambda j: (jnp.minimum((j + 1) * hpt, last), 0)),
            _const_spec(conv_w.shape),
            _const_spec(pool_w.shape),
            _const_spec(pool_s.shape),
            _layer_spec(wout.shape[1:], l),
        ],
        out_specs=pl.BlockSpec((tm, d), lambda j: (j, 0)),
        out_shape=jax.ShapeDtypeStruct((rows, d), F32),
        compiler_params=_params(1),
        name="mix",
    )(h, mods, attn, cp, cp, cp, conv_w, pool_w, pool_s, wout)


def _rope_tables(n_tokens):
    pos = jnp.arange(n_tokens)
    row = (pos // GRID_W).astype(F32)
    col = (pos % GRID_W).astype(F32)
    nfreq = ATTN_HEAD_DIM // 4
    inv = ROPE_BASE ** (-jnp.arange(nfreq, dtype=F32) / nfreq)
    ar, ac = row[:, None] * inv, col[:, None] * inv
    cos = jnp.concatenate([jnp.cos(ar), jnp.cos(ar), jnp.cos(ac), jnp.cos(ac)], axis=1)
    sin = jnp.concatenate([-jnp.sin(ar), jnp.sin(ar), -jnp.sin(ac), jnp.sin(ac)], axis=1)
    return jnp.tile(cos, (1, 2)), jnp.tile(sin, (1, 2))


def _block_diag(blocks):
    g, n, _ = blocks.shape
    eye = jnp.eye(g, dtype=blocks.dtype)
    return (eye[:, None, :, None] * blocks[:, :, None, :]).reshape(g * n, g * n)


def kernel(x, c, ctx, c_ctx, norm_w, w_mod, b_mod, ffn1_w_gu, ffn1_w_down, ffn2_w_gu, ffn2_w_down,
           w_in, w_out, q_norm_w, k_norm_w, lambda_qk, subln_w, conv_w, pool_w, pool_scale):
    batch, seq, d = x.shape
    ctx_len = ctx.shape[1]
    depth = w_mod.shape[0]
    n_heads_maps = ATTN_WIDTH // ATTN_HEAD_DIM

    cc = jnp.concatenate([c, c_ctx[None]], axis=0)
    mods_all = _mods_call(cc, w_mod, b_mod).reshape(depth, batch + 1, N_MOD, d)

    wgu1, wd1 = ffn1_w_gu.astype(BF16), ffn1_w_down.astype(BF16)
    wgu2, wd2 = ffn2_w_gu.astype(BF16), ffn2_w_down.astype(BF16)
    win, wout = w_in.astype(BF16), w_out.astype(BF16)
    gmat = _block_diag(jnp.ones((MXU_DIM // ATTN_HEAD_DIM, ATTN_HEAD_DIM, ATTN_HEAD_DIM), BF16))
    cos, sin = _rope_tables(seq)

    tm_lat, tm_ctx, tq_lat = 512, 256, 256
    lat_row = lambda j: j // (seq // tm_lat)
    ctx_row = lambda j: batch

    h = x.reshape(batch * seq, d)
    hc = ctx.reshape(batch * ctx_len, d)
    for l in range(depth):
        ctx_out = l < depth - 1
        lam_init = 0.8 - 0.6 * math.exp(-0.3 * l)
        mods = mods_all[l]
        nw = norm_w[l][:, None, :]
        qw = jnp.tile(q_norm_w[l], n_heads_maps)[None]
        kw = jnp.tile(k_norm_w[l], n_heads_maps)[None]
        pw = _block_diag(pool_w[l]).astype(BF16)
        ps = pool_scale[l][None]
        sw = subln_w[l][None]

        h = _ffn_call(h, mods, lat_row, nw[0], wgu1, wd1, l=l, i0=0, tm=tm_lat)
        hc = _ffn_call(hc, mods, ctx_row, nw[0], wgu1, wd1, l=l, i0=0, tm=tm_lat)

        q, k, v, cp = _proj_call(h, mods, lat_row, nw[1], win, gmat, qw, kw, cos, sin,
                                 l=l, tm=tm_lat, tiles_per_seq=seq // tm_lat)
        qc, kc, vc, cpc = _proj_call(hc, mods, ctx_row, nw[1], win, gmat, qw, kw, None, None,
                                     l=l, tm=tm_ctx, tiles_per_seq=1)

        attn = _attn_call(lambda_qk[l], sw, q, [k, kc], [v, vc], batch=batch, tq=tq_lat,
                          lam_init=lam_init)
        h = _mix_call(h, mods, lat_row, attn, cp, conv_w[l], pw, ps, wout,
                      l=l, tm=tm_lat, tiles_per_seq=seq // tm_lat)
        h = _ffn_call(h, mods, lat_row, nw[2], wgu2, wd2, l=l, i0=6, tm=tm_lat)
        if ctx_out:
            attn_c = _attn_call(lambda_qk[l], sw, qc, [kc], [vc], batch=batch, tq=ctx_len,
                                lam_init=lam_init)
            hc = _mix_call(hc, mods, ctx_row, attn_c, cpc, conv_w[l], pw, ps, wout,
                           l=l, tm=tm_ctx, tiles_per_seq=1)
            hc = _ffn_call(hc, mods, ctx_row, nw[2], wgu2, wd2, l=l, i0=6, tm=tm_lat)
    return h.reshape(batch, seq, d)
```

```python
import functools
import math

import jax
import jax.numpy as jnp
from jax import lax
from jax.experimental import pallas as pl
from jax.experimental.pallas import tpu as pltpu

F32 = jnp.float32
BF16 = jnp.bfloat16

GRID_W = 64
ATTN_HEADS = 4
ATTN_HEAD_DIM = 64
ATTN_V_DIM = 2 * ATTN_HEAD_DIM
ATTN_WIDTH = ATTN_HEADS * ATTN_V_DIM
POOL_WINDOWS = (2, 4, 8, 16)
ROPE_BASE = 10000.0
N_MOD = 9
EPS = 1e-6
Q_SCALE = ATTN_HEAD_DIM ** -0.5 * math.log2(math.e)

LANES = 128
SUBLANES = 8
HALO = SUBLANES
MXU_DIM = 256
VMEM_LIMIT = 56 * 1024 * 1024


def _const_spec(shape):
    zeros = (0,) * len(shape)
    return pl.BlockSpec(shape, lambda *_: zeros, pipeline_mode=pl.Buffered(1))


def _layer_spec(shape, l):
    idx = (l,) + (0,) * len(shape)
    return pl.BlockSpec((None,) + tuple(shape), lambda *_: idx, pipeline_mode=pl.Buffered(1))


def _params(n_axes):
    return pltpu.CompilerParams(dimension_semantics=("parallel",) * n_axes,
                                vmem_limit_bytes=VMEM_LIMIT)


def _rms_mod(x, nw, shift, scale):
    ms = jnp.mean(x * x, axis=-1, keepdims=True)
    y = x * lax.rsqrt(ms + EPS) * nw
    return y * (1.0 + scale) + shift


def _silu(x):
    return x * jax.nn.sigmoid(x)


def _mods_kernel(c_ref, w_ref, b_ref, o_ref):
    s = _silu(c_ref[...]).astype(BF16)
    o_ref[...] = jnp.dot(s, w_ref[...].astype(BF16), preferred_element_type=F32) + b_ref[...]


def _mods_call(cc, w_mod, b_mod):
    depth, d, n = w_mod.shape
    rows = cc.shape[0]
    tn = d
    return pl.pallas_call(
        _mods_kernel,
        grid=(depth, n // tn),
        in_specs=[
            pl.BlockSpec((rows, d), lambda l, j: (0, 0)),
            pl.BlockSpec((None, d, tn), lambda l, j: (l, 0, j)),
            pl.BlockSpec((None, 1, tn), lambda l, j: (l, 0, j)),
        ],
        out_specs=pl.BlockSpec((None, rows, tn), lambda l, j: (l, 0, j)),
        out_shape=jax.ShapeDtypeStruct((depth, rows, n), F32),
        compiler_params=_params(2),
        name="mods",
    )(cc, w_mod, b_mod.reshape(depth, 1, n))


def _ffn_step(x, mod_ref, i0, nw_ref, wgu_ref, wd_ref):
    d_ff = wd_ref.shape[0]
    xn = _rms_mod(x, nw_ref[...], mod_ref[i0:i0 + 1, :], mod_ref[i0 + 1:i0 + 2, :]).astype(BF16)
    gu = jnp.dot(xn, wgu_ref[...], preferred_element_type=F32)
    a = (_silu(gu[:, :d_ff]) * gu[:, d_ff:]).astype(BF16)
    f = jnp.dot(a, wd_ref[...], preferred_element_type=F32)
    return x + (0.5 * mod_ref[i0 + 2:i0 + 3, :]) * f


def _ffn_kernel(h_ref, mod_ref, nw_ref, wgu_ref, wd_ref, o_ref, *, i0):
    o_ref[...] = _ffn_step(h_ref[...], mod_ref, i0, nw_ref, wgu_ref, wd_ref)


def _ffn_call(h, mods, mod_row, nw, wgu, wd, *, l, i0, tm):
    rows, d = h.shape
    d_ff = wd.shape[1]
    return pl.pallas_call(
        functools.partial(_ffn_kernel, i0=i0),
        grid=(rows // tm,),
        in_specs=[
            pl.BlockSpec((tm, d), lambda j: (j, 0)),
            pl.BlockSpec((None, N_MOD, d), lambda j: (mod_row(j * tm), 0, 0)),
            _const_spec((1, d)),
            _layer_spec((d, 2 * d_ff), l),
            _layer_spec((d_ff, d), l),
        ],
        out_specs=pl.BlockSpec((tm, d), lambda j: (j, 0)),
        out_shape=jax.ShapeDtypeStruct((rows, d), F32),
        compiler_params=_params(1),
        name="ffn",
    )(h, mods, nw, wgu, wd)


def _swap_halves(z):
    lane = lax.broadcasted_iota(jnp.int32, z.shape, 1)
    up = pltpu.roll(z, LANES - 16, axis=1)
    down = pltpu.roll(z, 16, axis=1)
    return jnp.where((lane & 31) < 16, up, down)


def _head_norm(z, w_ref, g_ref):
    outs = []
    for s in range(z.shape[1] // MXU_DIM):
        zs = z[:, s * MXU_DIM:(s + 1) * MXU_DIM]
        ss = jnp.dot((zs * zs).astype(BF16), g_ref[...], preferred_element_type=F32)
        outs.append(zs * lax.rsqrt(ss * (1.0 / ATTN_HEAD_DIM) + EPS)
                    * w_ref[:, s * MXU_DIM:(s + 1) * MXU_DIM])
    return outs


def _rope(z, cos, sin):
    outs = []
    for s in range(z.shape[1] // LANES):
        zs = z[:, s * LANES:(s + 1) * LANES]
        outs.append(zs * cos + _swap_halves(zs) * sin)
    return jnp.concatenate(outs, axis=1)


def _proj_kernel(h_ref, mod_ref, nw_ref, win_ref, g_ref, qw_ref, kw_ref, *rest, rope):
    if rope:
        cos_ref, sin_ref, q_ref, k_ref, v_ref, cp_ref = rest
    else:
        q_ref, k_ref, v_ref, cp_ref = rest
    aw = ATTN_WIDTH
    cw = (win_ref.shape[1] - 3 * aw) // 4
    u = _rms_mod(h_ref[...], nw_ref[...], mod_ref[3:4, :], mod_ref[4:5, :]).astype(BF16)
    pqk = jnp.dot(u, win_ref[:, :2 * aw], preferred_element_type=F32)
    v_ref[...] = jnp.dot(u, win_ref[:, 2 * aw:3 * aw], preferred_element_type=F32).astype(BF16)
    normed = [_head_norm(pqk[:, :aw], qw_ref, g_ref), _head_norm(pqk[:, aw:], kw_ref, g_ref)]
    p = jnp.dot(u, win_ref[:, 3 * aw:], preferred_element_type=F32)
    for slabs, o_ref, sc in zip(normed, (q_ref, k_ref), (Q_SCALE, None)):
        if rope:
            slabs = [_rope(zs, cos_ref[...], sin_ref[...]) for zs in slabs]
        for s, zs in enumerate(slabs):
            if sc is not None:
                zs = zs * sc
            o_ref[:, s * MXU_DIM:(s + 1) * MXU_DIM] = zs.astype(BF16)
    cp_ref[:, :cw] = p[:, :cw]
    cp_ref[:, cw:2 * cw] = p[:, cw:2 * cw] * p[:, 2 * cw:3 * cw]
    cp_ref[:, 2 * cw:] = p[:, 3 * cw:]


def _proj_call(h, mods, mod_row, nw, win, gmat, qw, kw, cos, sin, *, l, tm, tiles_per_seq):
    rows, d = h.shape
    ncol = win.shape[2]
    aw = ATTN_WIDTH
    cw = (ncol - 3 * aw) // 4
    rope = cos is not None
    in_specs = [
        pl.BlockSpec((tm, d), lambda j: (j, 0)),
        pl.BlockSpec((None, N_MOD, d), lambda j: (mod_row(j * tm), 0, 0)),
        _const_spec((1, d)),
        _layer_spec((d, ncol), l),
        _const_spec((MXU_DIM, MXU_DIM)),
        _const_spec((1, aw)),
        _const_spec((1, aw)),
    ]
    args = [h, mods, nw, win, gmat, qw, kw]
    if rope:
        in_specs += [pl.BlockSpec((tm, LANES), lambda j: (j % tiles_per_seq, 0))] * 2
        args += [cos, sin]
    return pl.pallas_call(
        functools.partial(_proj_kernel, rope=rope),
        grid=(rows // tm,),
        in_specs=in_specs,
        out_specs=[pl.BlockSpec((tm, aw), lambda j: (j, 0))] * 3
        + [pl.BlockSpec((tm, 3 * cw), lambda j: (j, 0))],
        out_shape=[jax.ShapeDtypeStruct((rows, aw), BF16)] * 3
        + [jax.ShapeDtypeStruct((rows, 3 * cw), F32)],
        compiler_params=_params(1),
        name="proj",
    )(*args)


def _attn_kernel(lq_ref, sw_ref, q_ref, *rest, n_kv, tq, lam_init):
    kv_refs, o_ref = rest[:-1], rest[-1]
    k_refs, v_refs = kv_refs[:n_kv], kv_refs[n_kv:]
    lq = lq_ref[...]
    lam = (jnp.exp(jnp.sum(lq[0:1] * lq[1:2], axis=-1, keepdims=True))
           - jnp.exp(jnp.sum(lq[2:3] * lq[3:4], axis=-1, keepdims=True)) + lam_init)
    lane = lax.broadcasted_iota(jnp.int32, (tq, ATTN_V_DIM), 1)
    nt = (((1,), (1,)), ((), ()))
    va = [jnp.concatenate([v[...], jnp.ones(v.shape, BF16)], axis=1) for v in v_refs]

    def probs(i):
        q = q_ref[i * tq:(i + 1) * tq, :]
        q2 = jnp.concatenate([jnp.where(lane < ATTN_HEAD_DIM, q, jnp.zeros_like(q)),
                              jnp.where(lane >= ATTN_HEAD_DIM, q, jnp.zeros_like(q))], axis=0)
        s = [lax.dot_general(q2, k[...], nt, preferred_element_type=F32) for k in k_refs]
        m = functools.reduce(jnp.maximum, [jnp.max(x, axis=-1, keepdims=True) for x in s])
        return [jnp.exp2(x - m).astype(BF16) for x in s]

    def attend(i, e):
        acc = None
        for x, v in zip(e, va):
            pv = jnp.dot(x, v, preferred_element_type=F32)
            acc = pv if acc is None else acc + pv
        av = acc[:, :ATTN_V_DIM] / acc[:, ATTN_V_DIM:]
        o = av[:tq] - lam * av[tq:]
        ms = jnp.mean(o * o, axis=-1, keepdims=True)
        o_ref[i * tq:(i + 1) * tq, :] = (o * lax.rsqrt(ms + EPS) * sw_ref[...]
                                         * (1.0 - lam_init)).astype(BF16)

    nq = q_ref.shape[0] // tq
    e = probs(0)
    for i in range(nq):
        e_next = probs(i + 1) if i + 1 < nq else None
        attend(i, e)
        e = e_next


def _attn_call(lq, sw, q, ks, vs, *, batch, tq, lam_init):
    rows = q.shape[0]
    lq_len = rows // batch
    hd = ATTN_V_DIM
    kv_specs = [pl.BlockSpec((k.shape[0] // batch, hd), lambda b, h: (b, h)) for k in ks + vs]
    return pl.pallas_call(
        functools.partial(_attn_kernel, n_kv=len(ks), tq=tq, lam_init=lam_init),
        grid=(batch, ATTN_HEADS),
        in_specs=[
            _const_spec(lq.shape),
            _const_spec(sw.shape),
            pl.BlockSpec((lq_len, hd), lambda b, h: (b, h)),
        ] + kv_specs,
        out_specs=pl.BlockSpec((lq_len, hd), lambda b, h: (b, h)),
        out_shape=jax.ShapeDtypeStruct((rows, ATTN_WIDTH), BF16),
        compiler_params=_params(2),
        name="attn",
    )(lq, sw, q, *ks, *vs)


def _shift_rows(a, d):
    n = a.shape[0]
    return pltpu.roll(a, (-d) % n, axis=0)


def _mix_ffn_kernel(h_ref, mod_ref, attn_ref, cp_ref, prev_ref, next_ref, icnt_ref, cw_ref, pw_ref, ps_ref,
                    wout_ref, nw_ref, wgu_ref, wd_ref, o_ref, *, tiles_per_seq):
    tm = h_ref.shape[0]
    cw = cp_ref.shape[1] // 3
    aw = attn_ref.shape[1]
    y = jnp.dot(attn_ref[...], wout_ref[:aw, :], preferred_element_type=F32)
    jj = pl.program_id(0) % tiles_per_seq
    keep_prev = (jj > 0).astype(F32)
    keep_next = (jj < tiles_per_seq - 1).astype(F32)
    ext = jnp.concatenate([prev_ref[:, cw:] * keep_prev, cp_ref[:, cw:], next_ref[:, cw:] * keep_next],
                          axis=0)
    cur = slice(HALO, HALO + tm)
    uc = ext[:, :cw]
    conv = cp_ref[:, :cw] * (_shift_rows(uc, -1)[cur] * cw_ref[0:1, :] + uc[cur] * cw_ref[1:2, :]
                             + _shift_rows(uc, 1)[cur] * cw_ref[2:3, :])
    y = y + jnp.dot(conv.astype(BF16), wout_ref[aw:aw + cw, :], preferred_element_type=F32)
    lane = lax.broadcasted_iota(jnp.int32, (tm, LANES), 1)
    pooled = []
    for s in range(cw // LANES):
        x = ext[:, cw + s * LANES:cw + (s + 1) * LANES]
        s2 = x + _shift_rows(x, -1)
        s4 = _shift_rows(s2, -1) + _shift_rows(s2, 1)
        if s == 0:
            lo, hi = s2, s4
        else:
            s8 = _shift_rows(s4, -2) + _shift_rows(s4, 2)
            s16 = _shift_rows(s8, -4) + _shift_rows(s8, 4)
            lo, hi = s8, s16
        pooled.append(jnp.where(lane < 64, lo[cur], hi[cur]) * icnt_ref[:, s * LANES:(s + 1) * LANES]
                      - x[cur])
    pooled = jnp.concatenate(pooled, axis=1).astype(BF16)
    pool = jnp.dot(pooled, pw_ref[...], preferred_element_type=F32) * ps_ref[...]
    y = y + jnp.dot(pool.astype(BF16), wout_ref[aw + cw:, :], preferred_element_type=F32)
    h1 = h_ref[...] + mod_ref[5:6, :] * y
    o_ref[...] = _ffn_step(h1, mod_ref, 6, nw_ref, wgu_ref, wd_ref)


def _mix_ffn_call(h, mods, mod_row, attn, cp, icnt, conv_w, pool_w, pool_s, wout, nw, wgu, wd, *, l, tm):
    rows, d = h.shape
    aw = attn.shape[1]
    cw3 = cp.shape[1]
    d_ff = wd.shape[1]
    tiles_per_seq = icnt.shape[0] // tm
    hpt = tm // HALO
    last = rows // HALO - 1
    return pl.pallas_call(
        functools.partial(_mix_ffn_kernel, tiles_per_seq=tiles_per_seq),
        grid=(rows // tm,),
        in_specs=[
            pl.BlockSpec((tm, d), lambda j: (j, 0)),
            pl.BlockSpec((None, N_MOD, d), lambda j: (mod_row(j * tm), 0, 0)),
            pl.BlockSpec((tm, aw), lambda j: (j, 0)),
            pl.BlockSpec((tm, cw3), lambda j: (j, 0)),
            pl.BlockSpec((HALO, cw3), lambda j: (jnp.maximum(j * hpt - 1, 0), 0)),
            pl.BlockSpec((HALO, cw3), lambda j: (jnp.minimum((j + 1) * hpt, last), 0)),
            pl.BlockSpec((tm, icnt.shape[1]), lambda j: (j % tiles_per_seq, 0)),
            _const_spec(conv_w.shape),
            _const_spec(pool_w.shape),
            _const_spec(pool_s.shape),
            _layer_spec(wout.shape[1:], l),
            _const_spec((1, d)),
            _layer_spec((d, 2 * d_ff), l),
            _layer_spec((d_ff, d), l),
        ],
        out_specs=pl.BlockSpec((tm, d), lambda j: (j, 0)),
        out_shape=jax.ShapeDtypeStruct((rows, d), F32),
        compiler_params=_params(1),
        name="mix_ffn",
    )(h, mods, attn, cp, cp, cp, icnt, conv_w, pool_w, pool_s, wout, nw, wgu, wd)


def _rope_tables(n_tokens):
    pos = jnp.arange(n_tokens)
    row = (pos // GRID_W).astype(F32)
    col = (pos % GRID_W).astype(F32)
    nfreq = ATTN_HEAD_DIM // 4
    inv = ROPE_BASE ** (-jnp.arange(nfreq, dtype=F32) / nfreq)
    ar, ac = row[:, None] * inv, col[:, None] * inv
    cos = jnp.concatenate([jnp.cos(ar), jnp.cos(ar), jnp.cos(ac), jnp.cos(ac)], axis=1)
    sin = jnp.concatenate([-jnp.sin(ar), jnp.sin(ar), -jnp.sin(ac), jnp.sin(ac)], axis=1)
    return jnp.tile(cos, (1, 2)), jnp.tile(sin, (1, 2))


def _pool_inv_counts(seq_len, group_dim):
    t = jnp.arange(seq_len)[:, None]
    half = jnp.repeat(jnp.array([w // 2 for w in POOL_WINDOWS]), group_dim)[None, :]
    cnt = jnp.minimum(t + half, seq_len) - jnp.maximum(t - half, 0)
    return 1.0 / cnt.astype(F32)


def _block_diag(blocks):
    g, n, _ = blocks.shape
    eye = jnp.eye(g, dtype=blocks.dtype)
    return (eye[:, None, :, None] * blocks[:, :, None, :]).reshape(g * n, g * n)


def kernel(x, c, ctx, c_ctx, norm_w, w_mod, b_mod, ffn1_w_gu, ffn1_w_down, ffn2_w_gu, ffn2_w_down,
           w_in, w_out, q_norm_w, k_norm_w, lambda_qk, subln_w, conv_w, pool_w, pool_scale):
    batch, seq, d = x.shape
    ctx_len = ctx.shape[1]
    depth = w_mod.shape[0]
    n_heads_maps = ATTN_WIDTH // ATTN_HEAD_DIM

    cc = jnp.concatenate([c, c_ctx[None]], axis=0)
    mods_all = _mods_call(cc, w_mod, b_mod).reshape(depth, batch + 1, N_MOD, d)

    wgu1, wd1 = ffn1_w_gu.astype(BF16), ffn1_w_down.astype(BF16)
    wgu2, wd2 = ffn2_w_gu.astype(BF16), ffn2_w_down.astype(BF16)
    win, wout = w_in.astype(BF16), w_out.astype(BF16)
    gmat = _block_diag(jnp.ones((MXU_DIM // ATTN_HEAD_DIM, ATTN_HEAD_DIM, ATTN_HEAD_DIM), BF16))
    cos, sin = _rope_tables(seq)
    icnt, icnt_c = (_pool_inv_counts(n, pool_w.shape[-1]) for n in (seq, ctx_len))

    tm_ffn, tm_lat, tm_ctx, tq_lat = 1024, 512, 256, 256
    lat_row = lambda r: r // seq
    ctx_row = lambda r: batch

    h = x.reshape(batch * seq, d)
    hc = ctx.reshape(batch * ctx_len, d)
    for l in range(depth):
        ctx_out = l < depth - 1
        lam_init = 0.8 - 0.6 * math.exp(-0.3 * l)
        mods = mods_all[l]
        nw = norm_w[l][:, None, :]
        qw = jnp.tile(q_norm_w[l], n_heads_maps)[None]
        kw = jnp.tile(k_norm_w[l], n_heads_maps)[None]
        pw = _block_diag(pool_w[l]).astype(BF16)
        ps = pool_scale[l][None]
        sw = subln_w[l][None]

        h = _ffn_call(h, mods, lat_row, nw[0], wgu1, wd1, l=l, i0=0, tm=tm_ffn)
        hc = _ffn_call(hc, mods, ctx_row, nw[0], wgu1, wd1, l=l, i0=0, tm=tm_ffn)

        q, k, v, cp = _proj_call(h, mods, lat_row, nw[1], win, gmat, qw, kw, cos, sin,
                                 l=l, tm=tm_lat, tiles_per_seq=seq // tm_lat)
        qc, kc, vc, cpc = _proj_call(hc, mods, ctx_row, nw[1], win, gmat, qw, kw, None, None,
                                     l=l, tm=tm_ctx, tiles_per_seq=1)

        attn = _attn_call(lambda_qk[l], sw, q, [k, kc], [v, vc], batch=batch, tq=tq_lat,
                          lam_init=lam_init)
        h = _mix_ffn_call(h, mods, lat_row, attn, cp, icnt, conv_w[l], pw, ps, wout, nw[2], wgu2, wd2,
                          l=l, tm=tm_lat)
        if ctx_out:
            attn_c = _attn_call(lambda_qk[l], sw, qc, [kc], [vc], batch=batch, tq=ctx_len,
                                lam_init=lam_init)
            hc = _mix_ffn_call(hc, mods, ctx_row, attn_c, cpc, icnt_c, conv_w[l], pw, ps, wout, nw[2], wgu2,
                               wd2, l=l, tm=tm_ctx)
    return h.reshape(batch, seq, d)
```

```python
import functools
import math

import jax
import jax.numpy as jnp
from jax import lax
from jax.experimental import pallas as pl
from jax.experimental.pallas import tpu as pltpu

F32 = jnp.float32
BF16 = jnp.bfloat16

GRID_W = 64
ATTN_HEADS = 4
ATTN_HEAD_DIM = 64
ATTN_V_DIM = 2 * ATTN_HEAD_DIM
ATTN_WIDTH = ATTN_HEADS * ATTN_V_DIM
POOL_WINDOWS = (2, 4, 8, 16)
ROPE_BASE = 10000.0
N_MOD = 9
EPS = 1e-6
Q_SCALE = ATTN_HEAD_DIM ** -0.5 * math.log2(math.e)

LANES = 128
SUBLANES = 8
HALO = SUBLANES
MXU_DIM = 256
VMEM_LIMIT = 56 * 1024 * 1024


def _const_spec(shape):
    zeros = (0,) * len(shape)
    return pl.BlockSpec(shape, lambda *_: zeros, pipeline_mode=pl.Buffered(1))


def _params(n_axes):
    return pltpu.CompilerParams(dimension_semantics=("arbitrary",) * n_axes,
                                vmem_limit_bytes=VMEM_LIMIT)


def _rms_mod(x, nw, shift, scale):
    ms = jnp.mean(x * x, axis=-1, keepdims=True)
    y = x * lax.rsqrt(ms + EPS) * nw
    return y * (1.0 + scale) + shift


def _silu(x):
    return x * jax.nn.sigmoid(x)


def _mods_kernel(c_ref, w_ref, b_ref, o_ref):
    s = _silu(c_ref[...]).astype(BF16)
    o_ref[...] = jnp.dot(s, w_ref[...].astype(BF16), preferred_element_type=F32) + b_ref[...]


def _mods_call(cc, w_mod, b_mod):
    depth, d, n = w_mod.shape
    rows = cc.shape[0]
    tn = d
    return pl.pallas_call(
        _mods_kernel,
        grid=(depth, n // tn),
        in_specs=[
            pl.BlockSpec((rows, d), lambda l, j: (0, 0)),
            pl.BlockSpec((None, d, tn), lambda l, j: (l, 0, j)),
            pl.BlockSpec((None, 1, tn), lambda l, j: (l, 0, j)),
        ],
        out_specs=pl.BlockSpec((None, rows, tn), lambda l, j: (l, 0, j)),
        out_shape=jax.ShapeDtypeStruct((depth, rows, n), F32),
        compiler_params=_params(2),
        name="mods",
    )(cc, w_mod, b_mod.reshape(depth, 1, n))


def _ffn_step(x, mod_ref, i0, nw_ref, wgu_ref, wd_ref):
    d_ff = wd_ref.shape[0]
    xn = _rms_mod(x, nw_ref[...], mod_ref[i0:i0 + 1, :], mod_ref[i0 + 1:i0 + 2, :]).astype(BF16)
    gu = jnp.dot(xn, wgu_ref[...], preferred_element_type=F32)
    a = (_silu(gu[:, :d_ff]) * gu[:, d_ff:]).astype(BF16)
    f = jnp.dot(a, wd_ref[...], preferred_element_type=F32)
    return x + (0.5 * mod_ref[i0 + 2:i0 + 3, :]) * f


def _ffn_kernel(h_ref, mod_ref, nw_ref, wgu_ref, wd_ref, o_ref, *, i0):
    o_ref[...] = _ffn_step(h_ref[...], mod_ref, i0, nw_ref, wgu_ref, wd_ref)


def _ffn_call(h, mods, mod_row, nw, wgu, wd, *, i0, tm):
    rows, d = h.shape
    d_ff = wd.shape[0]
    return pl.pallas_call(
        functools.partial(_ffn_kernel, i0=i0),
        grid=(rows // tm,),
        in_specs=[
            pl.BlockSpec((tm, d), lambda j: (j, 0)),
            pl.BlockSpec((None, N_MOD, d), lambda j: (mod_row(j * tm), 0, 0)),
            _const_spec((1, d)),
            _const_spec((d, 2 * d_ff)),
            _const_spec((d_ff, d)),
        ],
        out_specs=pl.BlockSpec((tm, d), lambda j: (j, 0)),
        out_shape=jax.ShapeDtypeStruct((rows, d), F32),
        compiler_params=_params(1),
        name="ffn",
    )(h, mods, nw, wgu, wd)


def _swap_halves(z):
    lane = lax.broadcasted_iota(jnp.int32, z.shape, 1)
    up = pltpu.roll(z, LANES - 16, axis=1)
    down = pltpu.roll(z, 16, axis=1)
    return jnp.where((lane & 31) < 16, up, down)


def _head_norm(z, w_ref, g_ref):
    outs = []
    for s in range(z.shape[1] // MXU_DIM):
        zs = z[:, s * MXU_DIM:(s + 1) * MXU_DIM]
        ss = jnp.dot((zs * zs).astype(BF16), g_ref[...], preferred_element_type=F32)
        outs.append(zs * lax.rsqrt(ss * (1.0 / ATTN_HEAD_DIM) + EPS)
                    * w_ref[:, s * MXU_DIM:(s + 1) * MXU_DIM])
    return outs


def _rope(z, cos, sin):
    outs = []
    for s in range(z.shape[1] // LANES):
        zs = z[:, s * LANES:(s + 1) * LANES]
        outs.append(zs * cos + _swap_halves(zs) * sin)
    return jnp.concatenate(outs, axis=1)


def _proj_kernel(h_ref, mod_ref, nw_ref, win_ref, g_ref, qw_ref, kw_ref, *rest, rope):
    if rope:
        cos_ref, sin_ref, q_ref, k_ref, v_ref, cp_ref = rest
    else:
        q_ref, k_ref, v_ref, cp_ref = rest
    aw = ATTN_WIDTH
    cw = (win_ref.shape[1] - 3 * aw) // 4
    u = _rms_mod(h_ref[...], nw_ref[...], mod_ref[3:4, :], mod_ref[4:5, :]).astype(BF16)
    pqk = jnp.dot(u, win_ref[:, :2 * aw], preferred_element_type=F32)
    v_ref[...] = jnp.dot(u, win_ref[:, 2 * aw:3 * aw], preferred_element_type=F32).astype(BF16)
    normed = [_head_norm(pqk[:, :aw], qw_ref, g_ref), _head_norm(pqk[:, aw:], kw_ref, g_ref)]
    p = jnp.dot(u, win_ref[:, 3 * aw:], preferred_element_type=F32)
    for slabs, o_ref, sc in zip(normed, (q_ref, k_ref), (Q_SCALE, None)):
        if rope:
            slabs = [_rope(zs, cos_ref[...], sin_ref[...]) for zs in slabs]
        for s, zs in enumerate(slabs):
            if sc is not None:
                zs = zs * sc
            o_ref[:, s * MXU_DIM:(s + 1) * MXU_DIM] = zs.astype(BF16)
    cp_ref[:, :cw] = p[:, :cw]
    cp_ref[:, cw:2 * cw] = p[:, cw:2 * cw] * p[:, 2 * cw:3 * cw]
    cp_ref[:, 2 * cw:] = p[:, 3 * cw:]


def _proj_call(h, mods, mod_row, nw, win, gmat, qw, kw, cos, sin, *, tm, tiles_per_seq):
    rows, d = h.shape
    ncol = win.shape[1]
    aw = ATTN_WIDTH
    cw = (ncol - 3 * aw) // 4
    rope = cos is not None
    in_specs = [
        pl.BlockSpec((tm, d), lambda j: (j, 0)),
        pl.BlockSpec((None, N_MOD, d), lambda j: (mod_row(j * tm), 0, 0)),
        _const_spec((1, d)),
        _const_spec((d, ncol)),
        _const_spec((MXU_DIM, MXU_DIM)),
        _const_spec((1, aw)),
        _const_spec((1, aw)),
    ]
    args = [h, mods, nw, win, gmat, qw, kw]
    if rope:
        in_specs += [pl.BlockSpec((tm, LANES), lambda j: (j % tiles_per_seq, 0))] * 2
        args += [cos, sin]
    return pl.pallas_call(
        functools.partial(_proj_kernel, rope=rope),
        grid=(rows // tm,),
        in_specs=in_specs,
        out_specs=[pl.BlockSpec((tm, aw), lambda j: (j, 0))] * 3
        + [pl.BlockSpec((tm, 3 * cw), lambda j: (j, 0))],
        out_shape=[jax.ShapeDtypeStruct((rows, aw), BF16)] * 3
        + [jax.ShapeDtypeStruct((rows, 3 * cw), F32)],
        compiler_params=_params(1),
        name="proj",
    )(*args)


def _attn_kernel(lq_ref, swt_ref, q_ref, *rest, n_kv, n_cast, tq, lam_init):
    k_refs, v_refs = rest[:n_kv], rest[n_kv:2 * n_kv]
    cast_in, o_ref, cast_out = rest[2 * n_kv:2 * n_kv + n_cast], rest[2 * n_kv + n_cast], rest[2 * n_kv + n_cast + 1:]
    for src, dst in zip(cast_in, cast_out):
        dst[...] = src[...].astype(BF16)
    lq = lq_ref[...]
    lam = (jnp.exp(jnp.sum(lq[0:1] * lq[1:2], axis=-1, keepdims=True))
           - jnp.exp(jnp.sum(lq[2:3] * lq[3:4], axis=-1, keepdims=True)) + lam_init)
    lane = lax.broadcasted_iota(jnp.int32, (tq, ATTN_V_DIM), 1)
    nt = (((1,), (1,)), ((), ()))
    ones_rows = 2 * SUBLANES
    vta = [jnp.concatenate([v[...].astype(F32).T.astype(BF16), jnp.ones((ones_rows, v.shape[0]), BF16)],
                           axis=0) for v in v_refs]
    chunks = [(k, vt, r0) for k, vt in zip(k_refs, vta) for r0 in range(0, k.shape[0], MXU_DIM)]
    nq = q_ref.shape[0] // tq
    depth = 2
    e = {}
    for t in range(nq + depth):
        q2 = None
        if t < nq:
            q = q_ref[t * tq:(t + 1) * tq, :]
            q2 = jnp.concatenate([jnp.where(lane < ATTN_HEAD_DIM, q, jnp.zeros_like(q)),
                                  jnp.where(lane >= ATTN_HEAD_DIM, q, jnp.zeros_like(q))], axis=0)
        e_prev = e.pop(t - depth, None)
        s, acc = [], None
        for j, (k, vt, r0) in enumerate(chunks):
            if q2 is not None:
                s.append(lax.dot_general(k[r0:r0 + MXU_DIM, :], q2, nt, preferred_element_type=F32))
            if e_prev is not None:
                pv = jnp.dot(vt[:, r0:r0 + MXU_DIM], e_prev[j], preferred_element_type=F32)
                acc = pv if acc is None else acc + pv
        if q2 is not None:
            m = functools.reduce(jnp.maximum, [jnp.max(x, axis=0, keepdims=True) for x in s])
            e[t] = [jnp.exp2(x - m).astype(BF16) for x in s]
        if acc is not None:
            i = t - depth
            av = acc[:ATTN_V_DIM] / acc[ATTN_V_DIM:ATTN_V_DIM + 1]
            o = av[:, :tq] - lam * av[:, tq:]
            ms = jnp.mean(o * o, axis=0, keepdims=True)
            y = o * lax.rsqrt(ms + EPS) * swt_ref[...] * (1.0 - lam_init)
            o_ref[i * tq:(i + 1) * tq, :] = y.T.astype(BF16)


def _attn_call(lq, sw, q, ks, vs, *, batch, tq, lam_init, casts=()):
    rows = q.shape[0]
    lq_len = rows // batch
    hd = ATTN_V_DIM
    n_steps = batch * ATTN_HEADS
    swt = jnp.broadcast_to(sw.reshape(hd, 1), (hd, tq))
    kv_specs = [pl.BlockSpec((k.shape[0] // batch, hd), lambda b, h: (b, h)) for k in ks + vs]
    cast_in, cast_out, cast_shape = [], [], []
    for w, l in casts:
        _, r, c = w.shape
        n = n_steps
        while r % n or (r // n) % (2 * SUBLANES):
            n //= 2
        rep = n_steps // n
        cast_in.append(pl.BlockSpec((None, r // n, c),
                                    lambda b, h, l=l, rep=rep: (l, (b * ATTN_HEADS + h) // rep, 0)))
        cast_out.append(pl.BlockSpec((r // n, c), lambda b, h, rep=rep: ((b * ATTN_HEADS + h) // rep, 0)))
        cast_shape.append(jax.ShapeDtypeStruct((r, c), BF16))
    out = pl.pallas_call(
        functools.partial(_attn_kernel, n_kv=len(ks), n_cast=len(casts), tq=tq, lam_init=lam_init),
        grid=(batch, ATTN_HEADS),
        in_specs=[
            _const_spec(lq.shape),
            _const_spec(swt.shape),
            pl.BlockSpec((lq_len, hd), lambda b, h: (b, h)),
        ] + kv_specs + cast_in,
        out_specs=[pl.BlockSpec((lq_len, hd), lambda b, h: (b, h))] + cast_out,
        out_shape=[jax.ShapeDtypeStruct((rows, ATTN_WIDTH), BF16)] + cast_shape,
        compiler_params=_params(2),
        name="attn",
    )(lq, swt, q, *ks, *vs, *[w for w, _ in casts])
    return out[0], out[1:]


def _shift_rows(a, d):
    n = a.shape[0]
    return pltpu.roll(a, (-d) % n, axis=0)


def _mix_ffn_kernel(h_ref, mod_ref, attn_ref, cp_ref, prev_ref, next_ref, icnt_ref, cw_ref, pw_ref, ps_ref,
                    wout_ref, nw_ref, wgu_ref, wd_ref, o_ref, *, tiles_per_seq):
    tm = h_ref.shape[0]
    cw = cp_ref.shape[1] // 3
    aw = attn_ref.shape[1]
    y = jnp.dot(attn_ref[...], wout_ref[:aw, :], preferred_element_type=F32)
    jj = pl.program_id(0) % tiles_per_seq
    keep_prev = (jj > 0).astype(F32)
    keep_next = (jj < tiles_per_seq - 1).astype(F32)
    ext = jnp.concatenate([prev_ref[:, cw:] * keep_prev, cp_ref[:, cw:], next_ref[:, cw:] * keep_next],
                          axis=0)
    cur = slice(HALO, HALO + tm)
    uc = ext[:, :cw]
    conv = cp_ref[:, :cw] * (_shift_rows(uc, -1)[cur] * cw_ref[0:1, :] + uc[cur] * cw_ref[1:2, :]
                             + _shift_rows(uc, 1)[cur] * cw_ref[2:3, :])
    y = y + jnp.dot(conv.astype(BF16), wout_ref[aw:aw + cw, :], preferred_element_type=F32)
    lane = lax.broadcasted_iota(jnp.int32, (tm, LANES), 1)
    pooled = []
    for s in range(cw // LANES):
        x = ext[:, cw + s * LANES:cw + (s + 1) * LANES]
        s2 = x + _shift_rows(x, -1)
        s4 = _shift_rows(s2, -1) + _shift_rows(s2, 1)
        if s == 0:
            lo, hi = s2, s4
        else:
            s8 = _shift_rows(s4, -2) + _shift_rows(s4, 2)
            s16 = _shift_rows(s8, -4) + _shift_rows(s8, 4)
            lo, hi = s8, s16
        pooled.append(jnp.where(lane < 64, lo[cur], hi[cur]) * icnt_ref[:, s * LANES:(s + 1) * LANES]
                      - x[cur])
    pooled = jnp.concatenate(pooled, axis=1).astype(BF16)
    pool = jnp.dot(pooled, pw_ref[...], preferred_element_type=F32) * ps_ref[...]
    y = y + jnp.dot(pool.astype(BF16), wout_ref[aw + cw:, :], preferred_element_type=F32)
    h1 = h_ref[...] + mod_ref[5:6, :] * y
    o_ref[...] = _ffn_step(h1, mod_ref, 6, nw_ref, wgu_ref, wd_ref)


def _mix_ffn_call(h, mods, mod_row, attn, cp, icnt, conv_w, pool_w, pool_s, wout, nw, wgu, wd, *, tm):
    rows, d = h.shape
    aw = attn.shape[1]
    cw3 = cp.shape[1]
    d_ff = wd.shape[0]
    tiles_per_seq = icnt.shape[0] // tm
    hpt = tm // HALO
    last = rows // HALO - 1
    return pl.pallas_call(
        functools.partial(_mix_ffn_kernel, tiles_per_seq=tiles_per_seq),
        grid=(rows // tm,),
        in_specs=[
            pl.BlockSpec((tm, d), lambda j: (j, 0)),
            pl.BlockSpec((None, N_MOD, d), lambda j: (mod_row(j * tm), 0, 0)),
            pl.BlockSpec((tm, aw), lambda j: (j, 0)),
            pl.BlockSpec((tm, cw3), lambda j: (j, 0)),
            pl.BlockSpec((HALO, cw3), lambda j: (jnp.maximum(j * hpt - 1, 0), 0)),
            pl.BlockSpec((HALO, cw3), lambda j: (jnp.minimum((j + 1) * hpt, last), 0)),
            pl.BlockSpec((tm, icnt.shape[1]), lambda j: (j % tiles_per_seq, 0)),
            _const_spec(conv_w.shape),
            _const_spec(pool_w.shape),
            _const_spec(pool_s.shape),
            _const_spec(wout.shape),
            _const_spec((1, d)),
            _const_spec((d, 2 * d_ff)),
            _const_spec((d_ff, d)),
        ],
        out_specs=pl.BlockSpec((tm, d), lambda j: (j, 0)),
        out_shape=jax.ShapeDtypeStruct((rows, d), F32),
        compiler_params=_params(1),
        name="mix_ffn",
    )(h, mods, attn, cp, cp, cp, icnt, conv_w, pool_w, pool_s, wout, nw, wgu, wd)


def _rope_tables(n_tokens):
    pos = jnp.arange(n_tokens)
    row = (pos // GRID_W).astype(F32)
    col = (pos % GRID_W).astype(F32)
    nfreq = ATTN_HEAD_DIM // 4
    inv = ROPE_BASE ** (-jnp.arange(nfreq, dtype=F32) / nfreq)
    ar, ac = row[:, None] * inv, col[:, None] * inv
    cos = jnp.concatenate([jnp.cos(ar), jnp.cos(ar), jnp.cos(ac), jnp.cos(ac)], axis=1)
    sin = jnp.concatenate([-jnp.sin(ar), jnp.sin(ar), -jnp.sin(ac), jnp.sin(ac)], axis=1)
    return jnp.tile(cos, (1, 2)), jnp.tile(sin, (1, 2))


def _pool_inv_counts(seq_len, group_dim):
    t = jnp.arange(seq_len)[:, None]
    half = jnp.repeat(jnp.array([w // 2 for w in POOL_WINDOWS]), group_dim)[None, :]
    cnt = jnp.minimum(t + half, seq_len) - jnp.maximum(t - half, 0)
    return 1.0 / cnt.astype(F32)


def _block_diag(blocks):
    g, n, _ = blocks.shape
    eye = jnp.eye(g, dtype=blocks.dtype)
    return (eye[:, None, :, None] * blocks[:, :, None, :]).reshape(g * n, g * n)


def kernel(x, c, ctx, c_ctx, norm_w, w_mod, b_mod, ffn1_w_gu, ffn1_w_down, ffn2_w_gu, ffn2_w_down,
           w_in, w_out, q_norm_w, k_norm_w, lambda_qk, subln_w, conv_w, pool_w, pool_scale):
    batch, seq, d = x.shape
    ctx_len = ctx.shape[1]
    depth = w_mod.shape[0]
    n_heads_maps = ATTN_WIDTH // ATTN_HEAD_DIM

    cc = jnp.concatenate([c, c_ctx[None]], axis=0)
    mods_all = _mods_call(cc, w_mod, b_mod).reshape(depth, batch + 1, N_MOD, d)

    wgu1, wd1, win = (w[0].astype(BF16) for w in (ffn1_w_gu, ffn1_w_down, w_in))
    gmat = _block_diag(jnp.ones((MXU_DIM // ATTN_HEAD_DIM, ATTN_HEAD_DIM, ATTN_HEAD_DIM), BF16))
    cos, sin = _rope_tables(seq)
    icnt, icnt_c = (_pool_inv_counts(n, pool_w.shape[-1]) for n in (seq, ctx_len))

    tm_ffn, tm_lat, tm_ctx, tq_lat = 1024, 512, 256, 256
    lat_row = lambda r: r // seq
    ctx_row = lambda r: batch

    h = x.reshape(batch * seq, d)
    hc = ctx.reshape(batch * ctx_len, d)
    for l in range(depth):
        ctx_out = l < depth - 1
        lam_init = 0.8 - 0.6 * math.exp(-0.3 * l)
        mods = mods_all[l]
        nw = norm_w[l][:, None, :]
        qw = jnp.tile(q_norm_w[l], n_heads_maps)[None]
        kw = jnp.tile(k_norm_w[l], n_heads_maps)[None]
        pw = _block_diag(pool_w[l]).astype(BF16)
        ps = pool_scale[l][None]
        sw = subln_w[l]

        h = _ffn_call(h, mods, lat_row, nw[0], wgu1, wd1, i0=0, tm=tm_ffn)
        hc = _ffn_call(hc, mods, ctx_row, nw[0], wgu1, wd1, i0=0, tm=tm_ffn)

        q, k, v, cp = _proj_call(h, mods, lat_row, nw[1], win, gmat, qw, kw, cos, sin,
                                 tm=tm_lat, tiles_per_seq=seq // tm_lat)
        qc, kc, vc, cpc = _proj_call(hc, mods, ctx_row, nw[1], win, gmat, qw, kw, None, None,
                                     tm=tm_ctx, tiles_per_seq=1)

        casts = [(ffn2_w_gu, l), (ffn2_w_down, l), (w_out, l)]
        if ctx_out:
            casts += [(ffn1_w_gu, l + 1), (ffn1_w_down, l + 1), (w_in, l + 1)]
        attn, (wgu2, wd2, wout, *nxt) = _attn_call(lambda_qk[l], sw, q, [k, kc], [v, vc], batch=batch,
                                                   tq=tq_lat, lam_init=lam_init, casts=casts)
        h = _mix_ffn_call(h, mods, lat_row, attn, cp, icnt, conv_w[l], pw, ps, wout, nw[2], wgu2, wd2,
                          tm=tm_lat)
        if ctx_out:
            attn_c, _ = _attn_call(lambda_qk[l], sw, qc, [kc], [vc], batch=batch, tq=ctx_len,
                                   lam_init=lam_init)
            hc = _mix_ffn_call(hc, mods, ctx_row, attn_c, cpc, icnt_c, conv_w[l], pw, ps, wout, nw[2], wgu2,
                               wd2, tm=tm_ctx)
            wgu1, wd1, win = nxt
    return h.reshape(batch, seq, d)
```

```python
import functools
import math

import jax
import jax.numpy as jnp
from jax import lax
from jax.experimental import pallas as pl
from jax.experimental.pallas import tpu as pltpu

F32 = jnp.float32
BF16 = jnp.bfloat16

GRID_W = 64
ATTN_HEADS = 4
ATTN_HEAD_DIM = 64
ATTN_V_DIM = 2 * ATTN_HEAD_DIM
ATTN_WIDTH = ATTN_HEADS * ATTN_V_DIM
POOL_WINDOWS = (2, 4, 8, 16)
ROPE_BASE = 10000.0
N_MOD = 9
EPS = 1e-6
Q_SCALE = ATTN_HEAD_DIM ** -0.5 * math.log2(math.e)

LANES = 128
SUBLANES = 8
HALO = SUBLANES
MXU_DIM = 256
VMEM_LIMIT = 56 * 1024 * 1024


def _const_spec(shape):
    zeros = (0,) * len(shape)
    return pl.BlockSpec(shape, lambda *_: zeros, pipeline_mode=pl.Buffered(1))


def _params(n_axes):
    return pltpu.CompilerParams(dimension_semantics=("arbitrary",) * n_axes,
                                vmem_limit_bytes=VMEM_LIMIT)


def _rms_mod(x, nw, shift, scale):
    ms = jnp.mean(x * x, axis=-1, keepdims=True)
    y = x * lax.rsqrt(ms + EPS) * nw
    return y * (1.0 + scale) + shift


def _silu(x):
    return x * jax.nn.sigmoid(x)


def _mods_kernel(c_ref, w_ref, b_ref, o_ref):
    s = _silu(c_ref[...]).astype(BF16)
    o_ref[...] = jnp.dot(s, w_ref[...].astype(BF16), preferred_element_type=F32) + b_ref[...]


def _mods_call(cc, w_mod, b_mod):
    depth, d, n = w_mod.shape
    rows = cc.shape[0]
    tn = d
    return pl.pallas_call(
        _mods_kernel,
        grid=(depth, n // tn),
        in_specs=[
            pl.BlockSpec((rows, d), lambda l, j: (0, 0)),
            pl.BlockSpec((None, d, tn), lambda l, j: (l, 0, j)),
            pl.BlockSpec((None, 1, tn), lambda l, j: (l, 0, j)),
        ],
        out_specs=pl.BlockSpec((None, rows, tn), lambda l, j: (l, 0, j)),
        out_shape=jax.ShapeDtypeStruct((depth, rows, n), F32),
        compiler_params=_params(2),
        name="mods",
    )(cc, w_mod, b_mod.reshape(depth, 1, n))


def _ffn_step(x, mod_ref, i0, nw_ref, wgu_ref, wd_ref):
    d_ff = wd_ref.shape[0]
    xn = _rms_mod(x, nw_ref[...], mod_ref[i0:i0 + 1, :], mod_ref[i0 + 1:i0 + 2, :]).astype(BF16)
    gu = jnp.dot(xn, wgu_ref[...], preferred_element_type=F32)
    a = (_silu(gu[:, :d_ff]) * gu[:, d_ff:]).astype(BF16)
    f = jnp.dot(a, wd_ref[...], preferred_element_type=F32)
    return x + (0.5 * mod_ref[i0 + 2:i0 + 3, :]) * f


def _ffn_kernel(h_ref, mod_ref, nw_ref, wgu_ref, wd_ref, o_ref, *, i0):
    o_ref[...] = _ffn_step(h_ref[...], mod_ref, i0, nw_ref, wgu_ref, wd_ref)


def _ffn_call(h, mods, mod_row, nw, wgu, wd, *, i0, tm):
    rows, d = h.shape
    d_ff = wd.shape[0]
    return pl.pallas_call(
        functools.partial(_ffn_kernel, i0=i0),
        grid=(rows // tm,),
        in_specs=[
            pl.BlockSpec((tm, d), lambda j: (j, 0)),
            pl.BlockSpec((None, N_MOD, d), lambda j: (mod_row(j * tm), 0, 0)),
            _const_spec((1, d)),
            _const_spec((d, 2 * d_ff)),
            _const_spec((d_ff, d)),
        ],
        out_specs=pl.BlockSpec((tm, d), lambda j: (j, 0)),
        out_shape=jax.ShapeDtypeStruct((rows, d), F32),
        compiler_params=_params(1),
        name="ffn",
    )(h, mods, nw, wgu, wd)


def _swap_halves(z):
    lane = lax.broadcasted_iota(jnp.int32, z.shape, 1)
    up = pltpu.roll(z, LANES - 16, axis=1)
    down = pltpu.roll(z, 16, axis=1)
    return jnp.where((lane & 31) < 16, up, down)


def _head_norm(z, w_ref, g_ref):
    outs = []
    for s in range(z.shape[1] // MXU_DIM):
        zs = z[:, s * MXU_DIM:(s + 1) * MXU_DIM]
        ss = jnp.dot((zs * zs).astype(BF16), g_ref[...], preferred_element_type=F32)
        outs.append(zs * lax.rsqrt(ss * (1.0 / ATTN_HEAD_DIM) + EPS)
                    * w_ref[:, s * MXU_DIM:(s + 1) * MXU_DIM])
    return outs


def _rope(z, cos, sin):
    outs = []
    for s in range(z.shape[1] // LANES):
        zs = z[:, s * LANES:(s + 1) * LANES]
        outs.append(zs * cos + _swap_halves(zs) * sin)
    return jnp.concatenate(outs, axis=1)


def _proj_kernel(h_ref, mod_ref, nw_ref, win_ref, g_ref, qw_ref, kw_ref, *rest, rope):
    if rope:
        cos_ref, sin_ref, q_ref, k_ref, v_ref, cp_ref = rest
    else:
        q_ref, k_ref, v_ref, cp_ref = rest
    aw = ATTN_WIDTH
    cw = (win_ref.shape[1] - 3 * aw) // 4
    u = _rms_mod(h_ref[...], nw_ref[...], mod_ref[3:4, :], mod_ref[4:5, :]).astype(BF16)
    pqk = jnp.dot(u, win_ref[:, :2 * aw], preferred_element_type=F32)
    v_ref[...] = jnp.dot(u, win_ref[:, 2 * aw:3 * aw], preferred_element_type=F32).astype(BF16)
    normed = [_head_norm(pqk[:, :aw], qw_ref, g_ref), _head_norm(pqk[:, aw:], kw_ref, g_ref)]
    p = jnp.dot(u, win_ref[:, 3 * aw:], preferred_element_type=F32)
    for slabs, o_ref, sc in zip(normed, (q_ref, k_ref), (Q_SCALE, None)):
        if rope:
            slabs = [_rope(zs, cos_ref[...], sin_ref[...]) for zs in slabs]
        for s, zs in enumerate(slabs):
            if sc is not None:
                zs = zs * sc
            o_ref[:, s * MXU_DIM:(s + 1) * MXU_DIM] = zs.astype(BF16)
    cp_ref[:, :cw] = p[:, :cw]
    cp_ref[:, cw:2 * cw] = p[:, cw:2 * cw] * p[:, 2 * cw:3 * cw]
    cp_ref[:, 2 * cw:] = p[:, 3 * cw:]


def _proj_call(h, mods, mod_row, nw, win, gmat, qw, kw, cos, sin, *, tm, tiles_per_seq):
    rows, d = h.shape
    ncol = win.shape[1]
    aw = ATTN_WIDTH
    cw = (ncol - 3 * aw) // 4
    rope = cos is not None
    in_specs = [
        pl.BlockSpec((tm, d), lambda j: (j, 0)),
        pl.BlockSpec((None, N_MOD, d), lambda j: (mod_row(j * tm), 0, 0)),
        _const_spec((1, d)),
        _const_spec((d, ncol)),
        _const_spec((MXU_DIM, MXU_DIM)),
        _const_spec((1, aw)),
        _const_spec((1, aw)),
    ]
    args = [h, mods, nw, win, gmat, qw, kw]
    if rope:
        in_specs += [pl.BlockSpec((tm, LANES), lambda j: (j % tiles_per_seq, 0))] * 2
        args += [cos, sin]
    return pl.pallas_call(
        functools.partial(_proj_kernel, rope=rope),
        grid=(rows // tm,),
        in_specs=in_specs,
        out_specs=[pl.BlockSpec((tm, aw), lambda j: (j, 0))] * 3
        + [pl.BlockSpec((tm, 3 * cw), lambda j: (j, 0))],
        out_shape=[jax.ShapeDtypeStruct((rows, aw), BF16)] * 3
        + [jax.ShapeDtypeStruct((rows, 3 * cw), F32)],
        compiler_params=_params(1),
        name="proj",
    )(*args)


def _attn_kernel(lq_ref, sw_ref, q_ref, *rest, n_kv, n_cast, tq, lam_init):
    k_refs, v_refs = rest[:n_kv], rest[n_kv:2 * n_kv]
    cast_in, o_ref, cast_out = rest[2 * n_kv:2 * n_kv + n_cast], rest[2 * n_kv + n_cast], rest[2 * n_kv + n_cast + 1:]
    for src, dst in zip(cast_in, cast_out):
        dst[...] = src[...].astype(BF16)
    lq = lq_ref[...]
    lam = (jnp.exp(jnp.sum(lq[0:1] * lq[1:2], axis=-1, keepdims=True))
           - jnp.exp(jnp.sum(lq[2:3] * lq[3:4], axis=-1, keepdims=True)) + lam_init)
    lane = lax.broadcasted_iota(jnp.int32, (tq, ATTN_V_DIM), 1)
    nt = (((1,), (1,)), ((), ()))
    heads = q_ref.shape[1] // ATTN_V_DIM
    cols = [slice(hh * ATTN_V_DIM, (hh + 1) * ATTN_V_DIM) for hh in range(heads)]
    va = [[jnp.concatenate([v[:, c], jnp.ones((v.shape[0], ATTN_V_DIM), BF16)], axis=1) for v in v_refs]
          for c in cols]

    def probs(hh, i):
        q = q_ref[i * tq:(i + 1) * tq, cols[hh]]
        q2 = jnp.concatenate([jnp.where(lane < ATTN_HEAD_DIM, q, jnp.zeros_like(q)),
                              jnp.where(lane >= ATTN_HEAD_DIM, q, jnp.zeros_like(q))], axis=0)
        s = [lax.dot_general(q2, k[:, cols[hh]], nt, preferred_element_type=F32) for k in k_refs]
        m = functools.reduce(jnp.maximum, [jnp.max(x, axis=-1, keepdims=True) for x in s])
        return [jnp.exp2(x - m).astype(BF16) for x in s]

    def attend(hh, i, e):
        acc = None
        for x, v in zip(e, va[hh]):
            pv = jnp.dot(x, v, preferred_element_type=F32)
            acc = pv if acc is None else acc + pv
        av = acc[:, :ATTN_V_DIM] / acc[:, ATTN_V_DIM:]
        o = av[:tq] - lam * av[tq:]
        ms = jnp.mean(o * o, axis=-1, keepdims=True)
        o_ref[i * tq:(i + 1) * tq, cols[hh]] = (o * lax.rsqrt(ms + EPS) * sw_ref[...]
                                                * (1.0 - lam_init)).astype(BF16)

    units = [(hh, i) for hh in range(heads) for i in range(q_ref.shape[0] // tq)]
    e = probs(*units[0])
    for n, unit in enumerate(units):
        e_next = probs(*units[n + 1]) if n + 1 < len(units) else None
        attend(*unit, e)
        e = e_next


def _attn_call(lq, sw, q, ks, vs, *, batch, tq, heads, lam_init, casts=()):
    rows = q.shape[0]
    lq_len = rows // batch
    hd = ATTN_V_DIM * heads
    hsteps = ATTN_HEADS // heads
    n_steps = batch * hsteps
    kv_specs = [pl.BlockSpec((k.shape[0] // batch, hd), lambda b, h: (b, h)) for k in ks + vs]
    cast_in, cast_out, cast_shape = [], [], []
    for w, l in casts:
        _, r, c = w.shape
        n = n_steps
        while r % n or (r // n) % (2 * SUBLANES):
            n //= 2
        rep = n_steps // n
        cast_in.append(pl.BlockSpec((None, r // n, c),
                                    lambda b, h, l=l, rep=rep: (l, (b * hsteps + h) // rep, 0)))
        cast_out.append(pl.BlockSpec((r // n, c), lambda b, h, rep=rep: ((b * hsteps + h) // rep, 0)))
        cast_shape.append(jax.ShapeDtypeStruct((r, c), BF16))
    out = pl.pallas_call(
        functools.partial(_attn_kernel, n_kv=len(ks), n_cast=len(casts), tq=tq, lam_init=lam_init),
        grid=(batch, hsteps),
        in_specs=[
            _const_spec(lq.shape),
            _const_spec(sw.shape),
            pl.BlockSpec((lq_len, hd), lambda b, h: (b, h)),
        ] + kv_specs + cast_in,
        out_specs=[pl.BlockSpec((lq_len, hd), lambda b, h: (b, h))] + cast_out,
        out_shape=[jax.ShapeDtypeStruct((rows, ATTN_WIDTH), BF16)] + cast_shape,
        compiler_params=_params(2),
        name="attn",
    )(lq, sw, q, *ks, *vs, *[w for w, _ in casts])
    return out[0], out[1:]


def _shift_rows(a, d):
    n = a.shape[0]
    return pltpu.roll(a, (-d) % n, axis=0)


def _mix_ffn_kernel(h_ref, mod_ref, attn_ref, cp_ref, prev_ref, next_ref, icnt_ref, cw_ref, pw_ref, ps_ref,
                    wout_ref, nw_ref, wgu_ref, wd_ref, o_ref, *, tiles_per_seq):
    tm = h_ref.shape[0]
    cw = cp_ref.shape[1] // 3
    aw = attn_ref.shape[1]
    y = jnp.dot(attn_ref[...], wout_ref[:aw, :], preferred_element_type=F32)
    jj = pl.program_id(0) % tiles_per_seq
    keep_prev = (jj > 0).astype(F32)
    keep_next = (jj < tiles_per_seq - 1).astype(F32)
    ext = jnp.concatenate([prev_ref[:, cw:] * keep_prev, cp_ref[:, cw:], next_ref[:, cw:] * keep_next],
                          axis=0)
    cur = slice(HALO, HALO + tm)
    uc = ext[:, :cw]
    conv = cp_ref[:, :cw] * (_shift_rows(uc, -1)[cur] * cw_ref[0:1, :] + uc[cur] * cw_ref[1:2, :]
                             + _shift_rows(uc, 1)[cur] * cw_ref[2:3, :])
    y = y + jnp.dot(conv.astype(BF16), wout_ref[aw:aw + cw, :], preferred_element_type=F32)
    lane = lax.broadcasted_iota(jnp.int32, (tm, LANES), 1)
    pooled = []
    for s in range(cw // LANES):
        x = ext[:, cw + s * LANES:cw + (s + 1) * LANES]
        s2 = x + _shift_rows(x, -1)
        s4 = _shift_rows(s2, -1) + _shift_rows(s2, 1)
        if s == 0:
            lo, hi = s2, s4
        else:
            s8 = _shift_rows(s4, -2) + _shift_rows(s4, 2)
            s16 = _shift_rows(s8, -4) + _shift_rows(s8, 4)
            lo, hi = s8, s16
        pooled.append(jnp.where(lane < 64, lo[cur], hi[cur]) * icnt_ref[:, s * LANES:(s + 1) * LANES]
                      - x[cur])
    pooled = jnp.concatenate(pooled, axis=1).astype(BF16)
    pool = jnp.dot(pooled, pw_ref[...], preferred_element_type=F32) * ps_ref[...]
    y = y + jnp.dot(pool.astype(BF16), wout_ref[aw + cw:, :], preferred_element_type=F32)
    h1 = h_ref[...] + mod_ref[5:6, :] * y
    o_ref[...] = _ffn_step(h1, mod_ref, 6, nw_ref, wgu_ref, wd_ref)


def _mix_ffn_call(h, mods, mod_row, attn, cp, icnt, conv_w, pool_w, pool_s, wout, nw, wgu, wd, *, tm):
    rows, d = h.shape
    aw = attn.shape[1]
    cw3 = cp.shape[1]
    d_ff = wd.shape[0]
    tiles_per_seq = icnt.shape[0] // tm
    hpt = tm // HALO
    last = rows // HALO - 1
    return pl.pallas_call(
        functools.partial(_mix_ffn_kernel, tiles_per_seq=tiles_per_seq),
        grid=(rows // tm,),
        in_specs=[
            pl.BlockSpec((tm, d), lambda j: (j, 0)),
            pl.BlockSpec((None, N_MOD, d), lambda j: (mod_row(j * tm), 0, 0)),
            pl.BlockSpec((tm, aw), lambda j: (j, 0)),
            pl.BlockSpec((tm, cw3), lambda j: (j, 0)),
            pl.BlockSpec((HALO, cw3), lambda j: (jnp.maximum(j * hpt - 1, 0), 0)),
            pl.BlockSpec((HALO, cw3), lambda j: (jnp.minimum((j + 1) * hpt, last), 0)),
            pl.BlockSpec((tm, icnt.shape[1]), lambda j: (j % tiles_per_seq, 0)),
            _const_spec(conv_w.shape),
            _const_spec(pool_w.shape),
            _const_spec(pool_s.shape),
            _const_spec(wout.shape),
            _const_spec((1, d)),
            _const_spec((d, 2 * d_ff)),
            _const_spec((d_ff, d)),
        ],
        out_specs=pl.BlockSpec((tm, d), lambda j: (j, 0)),
        out_shape=jax.ShapeDtypeStruct((rows, d), F32),
        compiler_params=_params(1),
        name="mix_ffn",
    )(h, mods, attn, cp, cp, cp, icnt, conv_w, pool_w, pool_s, wout, nw, wgu, wd)


def _rope_tables(n_tokens):
    pos = jnp.arange(n_tokens)
    row = (pos // GRID_W).astype(F32)
    col = (pos % GRID_W).astype(F32)
    nfreq = ATTN_HEAD_DIM // 4
    inv = ROPE_BASE ** (-jnp.arange(nfreq, dtype=F32) / nfreq)
    ar, ac = row[:, None] * inv, col[:, None] * inv
    cos = jnp.concatenate([jnp.cos(ar), jnp.cos(ar), jnp.cos(ac), jnp.cos(ac)], axis=1)
    sin = jnp.concatenate([-jnp.sin(ar), jnp.sin(ar), -jnp.sin(ac), jnp.sin(ac)], axis=1)
    return jnp.tile(cos, (1, 2)), jnp.tile(sin, (1, 2))


def _pool_inv_counts(seq_len, group_dim):
    t = jnp.arange(seq_len)[:, None]
    half = jnp.repeat(jnp.array([w // 2 for w in POOL_WINDOWS]), group_dim)[None, :]
    cnt = jnp.minimum(t + half, seq_len) - jnp.maximum(t - half, 0)
    return 1.0 / cnt.astype(F32)


def _block_diag(blocks):
    g, n, _ = blocks.shape
    eye = jnp.eye(g, dtype=blocks.dtype)
    return (eye[:, None, :, None] * blocks[:, :, None, :]).reshape(g * n, g * n)


def kernel(x, c, ctx, c_ctx, norm_w, w_mod, b_mod, ffn1_w_gu, ffn1_w_down, ffn2_w_gu, ffn2_w_down,
           w_in, w_out, q_norm_w, k_norm_w, lambda_qk, subln_w, conv_w, pool_w, pool_scale):
    batch, seq, d = x.shape
    ctx_len = ctx.shape[1]
    depth = w_mod.shape[0]
    n_heads_maps = ATTN_WIDTH // ATTN_HEAD_DIM

    cc = jnp.concatenate([c, c_ctx[None]], axis=0)
    mods_all = _mods_call(cc, w_mod, b_mod).reshape(depth, batch + 1, N_MOD, d)

    wgu1, wd1, win = (w[0].astype(BF16) for w in (ffn1_w_gu, ffn1_w_down, w_in))
    gmat = _block_diag(jnp.ones((MXU_DIM // ATTN_HEAD_DIM, ATTN_HEAD_DIM, ATTN_HEAD_DIM), BF16))
    cos, sin = _rope_tables(seq)
    icnt, icnt_c = (_pool_inv_counts(n, pool_w.shape[-1]) for n in (seq, ctx_len))

    tm_ffn, tm_lat, tm_ctx, tq_lat = 1024, 512, 256, 256
    lat_row = lambda r: r // seq
    ctx_row = lambda r: batch

    h = x.reshape(batch * seq, d)
    hc = ctx.reshape(batch * ctx_len, d)
    for l in range(depth):
        ctx_out = l < depth - 1
        lam_init = 0.8 - 0.6 * math.exp(-0.3 * l)
        mods = mods_all[l]
        nw = norm_w[l][:, None, :]
        qw = jnp.tile(q_norm_w[l], n_heads_maps)[None]
        kw = jnp.tile(k_norm_w[l], n_heads_maps)[None]
        pw = _block_diag(pool_w[l]).astype(BF16)
        ps = pool_scale[l][None]
        sw = subln_w[l][None]

        h = _ffn_call(h, mods, lat_row, nw[0], wgu1, wd1, i0=0, tm=tm_ffn)
        hc = _ffn_call(hc, mods, ctx_row, nw[0], wgu1, wd1, i0=0, tm=tm_ffn)

        q, k, v, cp = _proj_call(h, mods, lat_row, nw[1], win, gmat, qw, kw, cos, sin,
                                 tm=tm_lat, tiles_per_seq=seq // tm_lat)
        qc, kc, vc, cpc = _proj_call(hc, mods, ctx_row, nw[1], win, gmat, qw, kw, None, None,
                                     tm=tm_lat, tiles_per_seq=1)

        casts = [(ffn2_w_gu, l), (ffn2_w_down, l), (w_out, l)]
        if ctx_out:
            casts += [(ffn1_w_gu, l + 1), (ffn1_w_down, l + 1), (w_in, l + 1)]
        attn, (wgu2, wd2, wout, *nxt) = _attn_call(lambda_qk[l], sw, q, [k, kc], [v, vc], batch=batch,
                                                   tq=tq_lat, heads=1, lam_init=lam_init, casts=casts)
        h = _mix_ffn_call(h, mods, lat_row, attn, cp, icnt, conv_w[l], pw, ps, wout, nw[2], wgu2, wd2,
                          tm=tm_lat)
        if ctx_out:
            attn_c, _ = _attn_call(lambda_qk[l], sw, qc, [kc], [vc], batch=batch, tq=ctx_len,
                                   heads=ATTN_HEADS, lam_init=lam_init)
            hc = _mix_ffn_call(hc, mods, ctx_row, attn_c, cpc, icnt_c, conv_w[l], pw, ps, wout, nw[2], wgu2,
                               wd2, tm=tm_ctx)
            wgu1, wd1, win = nxt
    return h.reshape(batch, seq, d)
```

```python
import functools
import math

import jax
import jax.numpy as jnp
from jax import lax
from jax.experimental import pallas as pl
from jax.experimental.pallas import tpu as pltpu

F32 = jnp.float32
BF16 = jnp.bfloat16

GRID_W = 64
ATTN_HEADS = 4
ATTN_HEAD_DIM = 64
ATTN_V_DIM = 2 * ATTN_HEAD_DIM
ATTN_WIDTH = ATTN_HEADS * ATTN_V_DIM
POOL_WINDOWS = (2, 4, 8, 16)
ROPE_BASE = 10000.0
N_MOD = 9
EPS = 1e-6
Q_SCALE = ATTN_HEAD_DIM ** -0.5 * math.log2(math.e)

LANES = 128
SUBLANES = 8
HALO = SUBLANES
MXU_DIM = 256
VMEM_LIMIT = 56 * 1024 * 1024


def _const_spec(shape):
    zeros = (0,) * len(shape)
    return pl.BlockSpec(shape, lambda *_: zeros, pipeline_mode=pl.Buffered(1))


def _params(n_axes):
    return pltpu.CompilerParams(dimension_semantics=("arbitrary",) * n_axes,
                                vmem_limit_bytes=VMEM_LIMIT)


def _rms_mod(x, nw, shift, scale):
    ms = jnp.mean(x * x, axis=-1, keepdims=True)
    y = x * lax.rsqrt(ms + EPS) * nw
    return y * (1.0 + scale) + shift


def _silu(x):
    return x * jax.nn.sigmoid(x)


def _mods_kernel(c_ref, w_ref, b_ref, o_ref):
    s = _silu(c_ref[...]).astype(BF16)
    o_ref[...] = jnp.dot(s, w_ref[...].astype(BF16), preferred_element_type=F32) + b_ref[...]


def _mods_call(cc, w_mod, b_mod):
    depth, d, n = w_mod.shape
    rows = cc.shape[0]
    tn = n // 2
    return pl.pallas_call(
        _mods_kernel,
        grid=(depth, n // tn),
        in_specs=[
            pl.BlockSpec((rows, d), lambda l, j: (0, 0)),
            pl.BlockSpec((None, d, tn), lambda l, j: (l, 0, j)),
            pl.BlockSpec((None, 1, tn), lambda l, j: (l, 0, j)),
        ],
        out_specs=pl.BlockSpec((None, rows, tn), lambda l, j: (l, 0, j)),
        out_shape=jax.ShapeDtypeStruct((depth, rows, n), F32),
        compiler_params=_params(2),
        name="mods",
    )(cc, w_mod, b_mod.reshape(depth, 1, n))


def _ffn_step(x, mod_ref, i0, nw_ref, wgu_ref, wd_ref):
    d_ff = wd_ref.shape[0]
    xn = _rms_mod(x, nw_ref[...], mod_ref[i0:i0 + 1, :], mod_ref[i0 + 1:i0 + 2, :]).astype(BF16)
    gu = jnp.dot(xn, wgu_ref[...], preferred_element_type=F32)
    a = (_silu(gu[:, :d_ff]) * gu[:, d_ff:]).astype(BF16)
    f = jnp.dot(a, wd_ref[...], preferred_element_type=F32)
    return x + (0.5 * mod_ref[i0 + 2:i0 + 3, :]) * f


def _ffn_kernel(h_ref, mod_ref, nw_ref, wgu_ref, wd_ref, o_ref, *, i0):
    o_ref[...] = _ffn_step(h_ref[...], mod_ref, i0, nw_ref, wgu_ref, wd_ref)


def _ffn_call(h, mods, mod_row, nw, wgu, wd, *, i0, tm):
    rows, d = h.shape
    d_ff = wd.shape[0]
    return pl.pallas_call(
        functools.partial(_ffn_kernel, i0=i0),
        grid=(rows // tm,),
        in_specs=[
            pl.BlockSpec((tm, d), lambda j: (j, 0)),
            pl.BlockSpec((None, N_MOD, d), lambda j: (mod_row(j * tm), 0, 0)),
            _const_spec((1, d)),
            _const_spec((d, 2 * d_ff)),
            _const_spec((d_ff, d)),
        ],
        out_specs=pl.BlockSpec((tm, d), lambda j: (j, 0)),
        out_shape=jax.ShapeDtypeStruct((rows, d), F32),
        compiler_params=_params(1),
        name="ffn",
    )(h, mods, nw, wgu, wd)


def _swap_halves(z):
    lane = lax.broadcasted_iota(jnp.int32, z.shape, 1)
    up = pltpu.roll(z, LANES - 16, axis=1)
    down = pltpu.roll(z, 16, axis=1)
    return jnp.where((lane & 31) < 16, up, down)


def _head_norm(z, w_ref, g_ref):
    outs = []
    for s in range(z.shape[1] // MXU_DIM):
        zs = z[:, s * MXU_DIM:(s + 1) * MXU_DIM]
        ss = jnp.dot((zs * zs).astype(BF16), g_ref[...], preferred_element_type=F32)
        outs.append(zs * lax.rsqrt(ss * (1.0 / ATTN_HEAD_DIM) + EPS)
                    * w_ref[:, s * MXU_DIM:(s + 1) * MXU_DIM])
    return outs


def _rope(z, cos, sin):
    outs = []
    for s in range(z.shape[1] // LANES):
        zs = z[:, s * LANES:(s + 1) * LANES]
        outs.append(zs * cos + _swap_halves(zs) * sin)
    return jnp.concatenate(outs, axis=1)


def _proj_kernel(h_ref, mod_ref, nw_ref, win_ref, g_ref, qw_ref, kw_ref, *rest, rope):
    if rope:
        cos_ref, sin_ref, q_ref, k_ref, v_ref, cp_ref = rest
    else:
        q_ref, k_ref, v_ref, cp_ref = rest
    aw = ATTN_WIDTH
    cw = (win_ref.shape[1] - 3 * aw) // 4
    u = _rms_mod(h_ref[...], nw_ref[...], mod_ref[3:4, :], mod_ref[4:5, :]).astype(BF16)
    pqk = jnp.dot(u, win_ref[:, :2 * aw], preferred_element_type=F32)
    v_ref[...] = jnp.dot(u, win_ref[:, 2 * aw:3 * aw], preferred_element_type=F32).astype(BF16)
    normed = [_head_norm(pqk[:, :aw], qw_ref, g_ref), _head_norm(pqk[:, aw:], kw_ref, g_ref)]
    p = jnp.dot(u, win_ref[:, 3 * aw:], preferred_element_type=F32)
    for slabs, o_ref, sc in zip(normed, (q_ref, k_ref), (Q_SCALE, None)):
        if rope:
            slabs = [_rope(zs, cos_ref[...], sin_ref[...]) for zs in slabs]
        for s, zs in enumerate(slabs):
            if sc is not None:
                zs = zs * sc
            o_ref[:, s * MXU_DIM:(s + 1) * MXU_DIM] = zs.astype(BF16)
    cp_ref[:, :cw] = p[:, :cw]
    cp_ref[:, cw:2 * cw] = p[:, cw:2 * cw] * p[:, 2 * cw:3 * cw]
    cp_ref[:, 2 * cw:] = p[:, 3 * cw:]


def _proj_call(h, mods, mod_row, nw, win, gmat, qw, kw, cos, sin, *, tm, tiles_per_seq):
    rows, d = h.shape
    ncol = win.shape[1]
    aw = ATTN_WIDTH
    cw = (ncol - 3 * aw) // 4
    rope = cos is not None
    in_specs = [
        pl.BlockSpec((tm, d), lambda j: (j, 0)),
        pl.BlockSpec((None, N_MOD, d), lambda j: (mod_row(j * tm), 0, 0)),
        _const_spec((1, d)),
        _const_spec((d, ncol)),
        _const_spec((MXU_DIM, MXU_DIM)),
        _const_spec((1, aw)),
        _const_spec((1, aw)),
    ]
    args = [h, mods, nw, win, gmat, qw, kw]
    if rope:
        in_specs += [pl.BlockSpec((tm, LANES), lambda j: (j % tiles_per_seq, 0))] * 2
        args += [cos, sin]
    return pl.pallas_call(
        functools.partial(_proj_kernel, rope=rope),
        grid=(rows // tm,),
        in_specs=in_specs,
        out_specs=[pl.BlockSpec((tm, aw), lambda j: (j, 0))] * 3
        + [pl.BlockSpec((tm, 3 * cw), lambda j: (j, 0))],
        out_shape=[jax.ShapeDtypeStruct((rows, aw), BF16)] * 3
        + [jax.ShapeDtypeStruct((rows, 3 * cw), F32)],
        compiler_params=_params(1),
        name="proj",
    )(*args)


def _attn_kernel(lq_ref, sw_ref, q_ref, *rest, n_kv, n_cast, tq, lam_init):
    k_refs, v_refs = rest[:n_kv], rest[n_kv:2 * n_kv]
    cast_in, o_ref, cast_out = rest[2 * n_kv:2 * n_kv + n_cast], rest[2 * n_kv + n_cast], rest[2 * n_kv + n_cast + 1:]
    for src, dst in zip(cast_in, cast_out):
        dst[...] = src[...].astype(BF16)
    lq = lq_ref[...]
    lam = (jnp.exp(jnp.sum(lq[0:1] * lq[1:2], axis=-1, keepdims=True))
           - jnp.exp(jnp.sum(lq[2:3] * lq[3:4], axis=-1, keepdims=True)) + lam_init)
    lane = lax.broadcasted_iota(jnp.int32, (tq, ATTN_V_DIM), 1)
    nt = (((1,), (1,)), ((), ()))
    heads = q_ref.shape[1] // ATTN_V_DIM
    cols = [slice(hh * ATTN_V_DIM, (hh + 1) * ATTN_V_DIM) for hh in range(heads)]
    va = [[jnp.concatenate([v[:, c], jnp.ones((v.shape[0], ATTN_V_DIM), BF16)], axis=1) for v in v_refs]
          for c in cols]

    def probs(hh, i):
        q = q_ref[i * tq:(i + 1) * tq, cols[hh]]
        q2 = jnp.concatenate([jnp.where(lane < ATTN_HEAD_DIM, q, jnp.zeros_like(q)),
                              jnp.where(lane >= ATTN_HEAD_DIM, q, jnp.zeros_like(q))], axis=0)
        s = [lax.dot_general(q2, k[:, cols[hh]], nt, preferred_element_type=F32) for k in k_refs]
        m = functools.reduce(jnp.maximum, [jnp.max(x, axis=-1, keepdims=True) for x in s])
        return [jnp.exp2(x - m).astype(BF16) for x in s]

    def attend(hh, i, e):
        acc = None
        for x, v in zip(e, va[hh]):
            pv = jnp.dot(x, v, preferred_element_type=F32)
            acc = pv if acc is None else acc + pv
        av = acc[:, :ATTN_V_DIM] / acc[:, ATTN_V_DIM:]
        o = av[:tq] - lam * av[tq:]
        ms = jnp.mean(o * o, axis=-1, keepdims=True)
        o_ref[i * tq:(i + 1) * tq, cols[hh]] = (o * lax.rsqrt(ms + EPS) * sw_ref[...]
                                                * (1.0 - lam_init)).astype(BF16)

    units = [(hh, i) for hh in range(heads) for i in range(q_ref.shape[0] // tq)]
    e = probs(*units[0])
    for n, unit in enumerate(units):
        e_next = probs(*units[n + 1]) if n + 1 < len(units) else None
        attend(*unit, e)
        e = e_next


def _attn_call(lq, sw, q, ks, vs, *, batch, tq, heads, lam_init, casts=()):
    rows = q.shape[0]
    lq_len = rows // batch
    hd = ATTN_V_DIM * heads
    hsteps = ATTN_HEADS // heads
    n_steps = batch * hsteps
    kv_specs = [pl.BlockSpec((k.shape[0] // batch, hd), lambda b, h: (b, h)) for k in ks + vs]
    cast_in, cast_out, cast_shape = [], [], []
    for w, l in casts:
        _, r, c = w.shape
        n = n_steps
        while r % n or (r // n) % (2 * SUBLANES):
            n //= 2
        rep = n_steps // n
        cast_in.append(pl.BlockSpec((None, r // n, c),
                                    lambda b, h, l=l, rep=rep: (l, (b * hsteps + h) // rep, 0)))
        cast_out.append(pl.BlockSpec((r // n, c), lambda b, h, rep=rep: ((b * hsteps + h) // rep, 0)))
        cast_shape.append(jax.ShapeDtypeStruct((r, c), BF16))
    out = pl.pallas_call(
        functools.partial(_attn_kernel, n_kv=len(ks), n_cast=len(casts), tq=tq, lam_init=lam_init),
        grid=(batch, hsteps),
        in_specs=[
            _const_spec(lq.shape),
            _const_spec(sw.shape),
            pl.BlockSpec((lq_len, hd), lambda b, h: (b, h)),
        ] + kv_specs + cast_in,
        out_specs=[pl.BlockSpec((lq_len, hd), lambda b, h: (b, h))] + cast_out,
        out_shape=[jax.ShapeDtypeStruct((rows, ATTN_WIDTH), BF16)] + cast_shape,
        compiler_params=_params(2),
        name="attn",
    )(lq, sw, q, *ks, *vs, *[w for w, _ in casts])
    return out[0], out[1:]


def _shift_rows(a, d):
    n = a.shape[0]
    return pltpu.roll(a, (-d) % n, axis=0)


def _mix_ffn_kernel(h_ref, mod_ref, attn_ref, cp_ref, prev_ref, next_ref, icnt_ref, cw_ref, pw_ref, ps_ref,
                    wout_ref, nw_ref, wgu_ref, wd_ref, o_ref, *, tiles_per_seq):
    tm = h_ref.shape[0]
    cw = cp_ref.shape[1] // 3
    aw = attn_ref.shape[1]
    y = jnp.dot(attn_ref[...], wout_ref[:aw, :], preferred_element_type=F32)
    jj = pl.program_id(0) % tiles_per_seq
    keep_prev = (jj > 0).astype(F32)
    keep_next = (jj < tiles_per_seq - 1).astype(F32)
    ext = jnp.concatenate([prev_ref[:, cw:] * keep_prev, cp_ref[:, cw:], next_ref[:, cw:] * keep_next],
                          axis=0)
    cur = slice(HALO, HALO + tm)
    uc = ext[:, :cw]
    conv = cp_ref[:, :cw] * (_shift_rows(uc, -1)[cur] * cw_ref[0:1, :] + uc[cur] * cw_ref[1:2, :]
                             + _shift_rows(uc, 1)[cur] * cw_ref[2:3, :])
    y = y + jnp.dot(conv.astype(BF16), wout_ref[aw:aw + cw, :], preferred_element_type=F32)
    lane = lax.broadcasted_iota(jnp.int32, (tm, LANES), 1)
    pooled = []
    for s in range(cw // LANES):
        x = ext[:, cw + s * LANES:cw + (s + 1) * LANES]
        s2 = x + _shift_rows(x, -1)
        s4 = _shift_rows(s2, -1) + _shift_rows(s2, 1)
        if s == 0:
            lo, hi = s2, s4
        else:
            s8 = _shift_rows(s4, -2) + _shift_rows(s4, 2)
            s16 = _shift_rows(s8, -4) + _shift_rows(s8, 4)
            lo, hi = s8, s16
        pooled.append(jnp.where(lane < 64, lo[cur], hi[cur]) * icnt_ref[:, s * LANES:(s + 1) * LANES]
                      - x[cur])
    pooled = jnp.concatenate(pooled, axis=1).astype(BF16)
    pool = jnp.dot(pooled, pw_ref[...], preferred_element_type=F32) * ps_ref[...]
    y = y + jnp.dot(pool.astype(BF16), wout_ref[aw + cw:, :], preferred_element_type=F32)
    h1 = h_ref[...] + mod_ref[5:6, :] * y
    o_ref[...] = _ffn_step(h1, mod_ref, 6, nw_ref, wgu_ref, wd_ref)


def _mix_ffn_call(h, mods, mod_row, attn, cp, icnt, conv_w, pool_w, pool_s, wout, nw, wgu, wd, *, tm):
    rows, d = h.shape
    aw = attn.shape[1]
    cw3 = cp.shape[1]
    d_ff = wd.shape[0]
    tiles_per_seq = icnt.shape[0] // tm
    hpt = tm // HALO
    last = rows // HALO - 1
    return pl.pallas_call(
        functools.partial(_mix_ffn_kernel, tiles_per_seq=tiles_per_seq),
        grid=(rows // tm,),
        in_specs=[
            pl.BlockSpec((tm, d), lambda j: (j, 0)),
            pl.BlockSpec((None, N_MOD, d), lambda j: (mod_row(j * tm), 0, 0)),
            pl.BlockSpec((tm, aw), lambda j: (j, 0)),
            pl.BlockSpec((tm, cw3), lambda j: (j, 0)),
            pl.BlockSpec((HALO, cw3), lambda j: (jnp.maximum(j * hpt - 1, 0), 0)),
            pl.BlockSpec((HALO, cw3), lambda j: (jnp.minimum((j + 1) * hpt, last), 0)),
            pl.BlockSpec((tm, icnt.shape[1]), lambda j: (j % tiles_per_seq, 0)),
            _const_spec(conv_w.shape),
            _const_spec(pool_w.shape),
            _const_spec(pool_s.shape),
            _const_spec(wout.shape),
            _const_spec((1, d)),
            _const_spec((d, 2 * d_ff)),
            _const_spec((d_ff, d)),
        ],
        out_specs=pl.BlockSpec((tm, d), lambda j: (j, 0)),
        out_shape=jax.ShapeDtypeStruct((rows, d), F32),
        compiler_params=_params(1),
        name="mix_ffn",
    )(h, mods, attn, cp, cp, cp, icnt, conv_w, pool_w, pool_s, wout, nw, wgu, wd)


def _rope_tables(n_tokens):
    pos = jnp.arange(n_tokens)
    row = (pos // GRID_W).astype(F32)
    col = (pos % GRID_W).astype(F32)
    nfreq = ATTN_HEAD_DIM // 4
    inv = ROPE_BASE ** (-jnp.arange(nfreq, dtype=F32) / nfreq)
    ar, ac = row[:, None] * inv, col[:, None] * inv
    cos = jnp.concatenate([jnp.cos(ar), jnp.cos(ar), jnp.cos(ac), jnp.cos(ac)], axis=1)
    sin = jnp.concatenate([-jnp.sin(ar), jnp.sin(ar), -jnp.sin(ac), jnp.sin(ac)], axis=1)
    return jnp.tile(cos, (1, 2)), jnp.tile(sin, (1, 2))


def _pool_inv_counts(seq_len, group_dim):
    t = jnp.arange(seq_len)[:, None]
    half = jnp.repeat(jnp.array([w // 2 for w in POOL_WINDOWS]), group_dim)[None, :]
    cnt = jnp.minimum(t + half, seq_len) - jnp.maximum(t - half, 0)
    return 1.0 / cnt.astype(F32)


def _block_diag(blocks):
    g, n, _ = blocks.shape
    eye = jnp.eye(g, dtype=blocks.dtype)
    return (eye[:, None, :, None] * blocks[:, :, None, :]).reshape(g * n, g * n)


def kernel(x, c, ctx, c_ctx, norm_w, w_mod, b_mod, ffn1_w_gu, ffn1_w_down, ffn2_w_gu, ffn2_w_down,
           w_in, w_out, q_norm_w, k_norm_w, lambda_qk, subln_w, conv_w, pool_w, pool_scale):
    batch, seq, d = x.shape
    ctx_len = ctx.shape[1]
    depth = w_mod.shape[0]
    n_heads_maps = ATTN_WIDTH // ATTN_HEAD_DIM

    cc = jnp.concatenate([c, c_ctx[None]], axis=0)
    mods_all = _mods_call(cc, w_mod, b_mod).reshape(depth, batch + 1, N_MOD, d)

    wgu1, wd1, win = (w[0].astype(BF16) for w in (ffn1_w_gu, ffn1_w_down, w_in))
    gmat = _block_diag(jnp.ones((MXU_DIM // ATTN_HEAD_DIM, ATTN_HEAD_DIM, ATTN_HEAD_DIM), BF16))
    cos, sin = _rope_tables(seq)
    icnt, icnt_c = (_pool_inv_counts(n, pool_w.shape[-1]) for n in (seq, ctx_len))

    tm_ffn, tm_lat, tm_ctx, tq_lat = 1024, 512, 256, 256
    lat_row = lambda r: r // seq
    ctx_row = lambda r: batch

    h = x.reshape(batch * seq, d)
    hc = ctx.reshape(batch * ctx_len, d)
    for l in range(depth):
        ctx_out = l < depth - 1
        lam_init = 0.8 - 0.6 * math.exp(-0.3 * l)
        mods = mods_all[l]
        nw = norm_w[l][:, None, :]
        qw = jnp.tile(q_norm_w[l], n_heads_maps)[None]
        kw = jnp.tile(k_norm_w[l], n_heads_maps)[None]
        pw = _block_diag(pool_w[l]).astype(BF16)
        ps = pool_scale[l][None]
        sw = subln_w[l][None]

        h = _ffn_call(h, mods, lat_row, nw[0], wgu1, wd1, i0=0, tm=tm_ffn)
        hc = _ffn_call(hc, mods, ctx_row, nw[0], wgu1, wd1, i0=0, tm=tm_ffn)

        q, k, v, cp = _proj_call(h, mods, lat_row, nw[1], win, gmat, qw, kw, cos, sin,
                                 tm=tm_ffn, tiles_per_seq=seq // tm_ffn)
        qc, kc, vc, cpc = _proj_call(hc, mods, ctx_row, nw[1], win, gmat, qw, kw, None, None,
                                     tm=tm_ffn, tiles_per_seq=1)

        casts = [(ffn2_w_gu, l), (ffn2_w_down, l), (w_out, l)]
        if ctx_out:
            casts += [(ffn1_w_gu, l + 1), (ffn1_w_down, l + 1), (w_in, l + 1)]
        attn, (wgu2, wd2, wout, *nxt) = _attn_call(lambda_qk[l], sw, q, [k, kc], [v, vc], batch=batch,
                                                   tq=tq_lat, heads=1, lam_init=lam_init, casts=casts)
        h = _mix_ffn_call(h, mods, lat_row, attn, cp, icnt, conv_w[l], pw, ps, wout, nw[2], wgu2, wd2,
                          tm=tm_lat)
        if ctx_out:
            attn_c, _ = _attn_call(lambda_qk[l], sw, qc, [kc], [vc], batch=batch, tq=ctx_len,
                                   heads=ATTN_HEADS, lam_init=lam_init)
            hc = _mix_ffn_call(hc, mods, ctx_row, attn_c, cpc, icnt_c, conv_w[l], pw, ps, wout, nw[2], wgu2,
                               wd2, tm=tm_ctx)
            wgu1, wd1, win = nxt
    return h.reshape(batch, seq, d)
```

```python
import functools
import math

import jax
import jax.numpy as jnp
from jax import lax
from jax.experimental import pallas as pl
from jax.experimental.pallas import tpu as pltpu

F32 = jnp.float32
BF16 = jnp.bfloat16

GRID_W = 64
ATTN_HEADS = 4
ATTN_HEAD_DIM = 64
ATTN_V_DIM = 2 * ATTN_HEAD_DIM
ATTN_WIDTH = ATTN_HEADS * ATTN_V_DIM
POOL_WINDOWS = (2, 4, 8, 16)
ROPE_BASE = 10000.0
N_MOD = 9
EPS = 1e-6
ROPE_PAIR = ATTN_HEAD_DIM // 4
Q_SCALE = ATTN_HEAD_DIM ** -0.5 * math.log2(math.e)

LANES = 128
SUBLANES = 8
HALO = SUBLANES
MXU_DIM = 256
VMEM_LIMIT = 56 * 1024 * 1024


def _const_spec(shape):
    zeros = (0,) * len(shape)
    return pl.BlockSpec(shape, lambda *_: zeros, pipeline_mode=pl.Buffered(1))


def _params(n_axes):
    return pltpu.CompilerParams(dimension_semantics=("arbitrary",) * n_axes,
                                vmem_limit_bytes=VMEM_LIMIT)


def _rms_mod(x, nw, shift, scale):
    ms = jnp.mean(x * x, axis=-1, keepdims=True)
    y = x * lax.rsqrt(ms + EPS) * nw
    return y * (1.0 + scale) + shift


def _silu(x):
    return x * jax.nn.sigmoid(x)


def _mods_kernel(c_ref, w_ref, b_ref, o_ref):
    s = _silu(c_ref[...]).astype(BF16)
    o_ref[...] = jnp.dot(s, w_ref[...].astype(BF16), preferred_element_type=F32) + b_ref[...]


def _mods_call(cc, w_mod, b_mod):
    depth, d, n = w_mod.shape
    rows = cc.shape[0]
    tn = n // 2
    return pl.pallas_call(
        _mods_kernel,
        grid=(depth, n // tn),
        in_specs=[
            pl.BlockSpec((rows, d), lambda l, j: (0, 0)),
            pl.BlockSpec((None, d, tn), lambda l, j: (l, 0, j)),
            pl.BlockSpec((None, 1, tn), lambda l, j: (l, 0, j)),
        ],
        out_specs=pl.BlockSpec((None, rows, tn), lambda l, j: (l, 0, j)),
        out_shape=jax.ShapeDtypeStruct((depth, rows, n), F32),
        compiler_params=_params(2),
        name="mods",
    )(cc, w_mod, b_mod.reshape(depth, 1, n))


def _ffn_step(x, mod_ref, i0, nw_ref, wgu_ref, wd_ref):
    d_ff = wd_ref.shape[0]
    xn = _rms_mod(x, nw_ref[...], mod_ref[i0:i0 + 1, :], mod_ref[i0 + 1:i0 + 2, :]).astype(BF16)
    gu = jnp.dot(xn, wgu_ref[...], preferred_element_type=F32)
    a = (_silu(gu[:, :d_ff]) * gu[:, d_ff:]).astype(BF16)
    f = jnp.dot(a, wd_ref[...], preferred_element_type=F32)
    return x + (0.5 * mod_ref[i0 + 2:i0 + 3, :]) * f


def _ffn_kernel(h_ref, mod_ref, nw_ref, wgu_ref, wd_ref, o_ref, *, i0):
    o_ref[...] = _ffn_step(h_ref[...], mod_ref, i0, nw_ref, wgu_ref, wd_ref)


def _ffn_call(h, mods, mod_row, nw, wgu, wd, *, i0, tm):
    rows, d = h.shape
    d_ff = wd.shape[0]
    return pl.pallas_call(
        functools.partial(_ffn_kernel, i0=i0),
        grid=(rows // tm,),
        in_specs=[
            pl.BlockSpec((tm, d), lambda j: (j, 0)),
            pl.BlockSpec((None, N_MOD, d), lambda j: (mod_row(j * tm), 0, 0)),
            _const_spec((1, d)),
            _const_spec((d, 2 * d_ff)),
            _const_spec((d_ff, d)),
        ],
        out_specs=pl.BlockSpec((tm, d), lambda j: (j, 0)),
        out_shape=jax.ShapeDtypeStruct((rows, d), F32),
        compiler_params=_params(1),
        name="ffn",
    )(h, mods, nw, wgu, wd)


def _swap_halves(z):
    lane = lax.broadcasted_iota(jnp.int32, z.shape, 1)
    up = pltpu.roll(z, LANES - ROPE_PAIR, axis=1)
    down = pltpu.roll(z, ROPE_PAIR, axis=1)
    return jnp.where((lane & (2 * ROPE_PAIR - 1)) < ROPE_PAIR, up, down)


def _head_norm(z, w_ref, g_ref):
    outs = []
    for s in range(z.shape[1] // MXU_DIM):
        zs = z[:, s * MXU_DIM:(s + 1) * MXU_DIM]
        ss = jnp.dot((zs * zs).astype(BF16), g_ref[...], preferred_element_type=F32)
        outs.append(zs * lax.rsqrt(ss * (1.0 / ATTN_HEAD_DIM) + EPS)
                    * w_ref[:, s * MXU_DIM:(s + 1) * MXU_DIM])
    return outs


def _rope(z, cos, sin):
    outs = []
    for s in range(z.shape[1] // LANES):
        zs = z[:, s * LANES:(s + 1) * LANES]
        outs.append(zs * cos + _swap_halves(zs) * sin)
    return jnp.concatenate(outs, axis=1)


def _proj_kernel(h_ref, mod_ref, nw_ref, win_ref, g_ref, qw_ref, kw_ref, *rest, rope):
    if rope:
        cos_ref, sin_ref, q_ref, k_ref, v_ref, cp_ref = rest
    else:
        q_ref, k_ref, v_ref, cp_ref = rest
    aw = ATTN_WIDTH
    cw = (win_ref.shape[1] - 3 * aw) // 4
    u = _rms_mod(h_ref[...], nw_ref[...], mod_ref[3:4, :], mod_ref[4:5, :]).astype(BF16)
    pqk = jnp.dot(u, win_ref[:, :2 * aw], preferred_element_type=F32)
    v_ref[...] = jnp.dot(u, win_ref[:, 2 * aw:3 * aw], preferred_element_type=F32).astype(BF16)
    normed = [_head_norm(pqk[:, :aw], qw_ref, g_ref), _head_norm(pqk[:, aw:], kw_ref, g_ref)]
    p = jnp.dot(u, win_ref[:, 3 * aw:], preferred_element_type=F32)
    for slabs, o_ref, sc in zip(normed, (q_ref, k_ref), (Q_SCALE, None)):
        if rope:
            slabs = [_rope(zs, cos_ref[...], sin_ref[...]) for zs in slabs]
        for s, zs in enumerate(slabs):
            if sc is not None:
                zs = zs * sc
            o_ref[:, s * MXU_DIM:(s + 1) * MXU_DIM] = zs.astype(BF16)
    cp_ref[:, :cw] = p[:, :cw]
    cp_ref[:, cw:2 * cw] = p[:, cw:2 * cw] * p[:, 2 * cw:3 * cw]
    cp_ref[:, 2 * cw:] = p[:, 3 * cw:]


def _proj_call(h, mods, mod_row, nw, win, gmat, qw, kw, cos, sin, *, tm, tiles_per_seq):
    rows, d = h.shape
    ncol = win.shape[1]
    aw = ATTN_WIDTH
    cw = (ncol - 3 * aw) // 4
    rope = cos is not None
    in_specs = [
        pl.BlockSpec((tm, d), lambda j: (j, 0)),
        pl.BlockSpec((None, N_MOD, d), lambda j: (mod_row(j * tm), 0, 0)),
        _const_spec((1, d)),
        _const_spec((d, ncol)),
        _const_spec((MXU_DIM, MXU_DIM)),
        _const_spec((1, aw)),
        _const_spec((1, aw)),
    ]
    args = [h, mods, nw, win, gmat, qw, kw]
    if rope:
        in_specs += [pl.BlockSpec((tm, LANES), lambda j: (j % tiles_per_seq, 0))] * 2
        args += [cos, sin]
    return pl.pallas_call(
        functools.partial(_proj_kernel, rope=rope),
        grid=(rows // tm,),
        in_specs=in_specs,
        out_specs=[pl.BlockSpec((tm, aw), lambda j: (j, 0))] * 3
        + [pl.BlockSpec((tm, 3 * cw), lambda j: (j, 0))],
        out_shape=[jax.ShapeDtypeStruct((rows, aw), BF16)] * 3
        + [jax.ShapeDtypeStruct((rows, 3 * cw), F32)],
        compiler_params=_params(1),
        name="proj",
    )(*args)


def _attn_kernel(lq_ref, sw_ref, q_ref, *rest, n_kv, n_cast, tq, lam_init):
    k_refs, v_refs = rest[:n_kv], rest[n_kv:2 * n_kv]
    cast_in, o_ref, cast_out = rest[2 * n_kv:2 * n_kv + n_cast], rest[2 * n_kv + n_cast], rest[2 * n_kv + n_cast + 1:]
    for src, dst in zip(cast_in, cast_out):
        dst[...] = src[...].astype(BF16)
    lq = lq_ref[...]
    lam = (jnp.exp(jnp.sum(lq[0:1] * lq[1:2], axis=-1, keepdims=True))
           - jnp.exp(jnp.sum(lq[2:3] * lq[3:4], axis=-1, keepdims=True)) + lam_init)
    lane = lax.broadcasted_iota(jnp.int32, (tq, ATTN_V_DIM), 1)
    nt = (((1,), (1,)), ((), ()))
    heads = q_ref.shape[1] // ATTN_V_DIM
    cols = [slice(hh * ATTN_V_DIM, (hh + 1) * ATTN_V_DIM) for hh in range(heads)]
    va = [[jnp.concatenate([v[:, c], jnp.ones((v.shape[0], ATTN_V_DIM), BF16)], axis=1) for v in v_refs]
          for c in cols]

    def probs(hh, i):
        q = q_ref[i * tq:(i + 1) * tq, cols[hh]]
        q2 = jnp.concatenate([jnp.where(lane < ATTN_HEAD_DIM, q, jnp.zeros_like(q)),
                              jnp.where(lane >= ATTN_HEAD_DIM, q, jnp.zeros_like(q))], axis=0)
        s = [lax.dot_general(q2, k[:, cols[hh]], nt, preferred_element_type=F32) for k in k_refs]
        m = functools.reduce(jnp.maximum, [jnp.max(x, axis=-1, keepdims=True) for x in s])
        return [jnp.exp2(x - m).astype(BF16) for x in s]

    def attend(hh, i, e):
        acc = None
        for x, v in zip(e, va[hh]):
            pv = jnp.dot(x, v, preferred_element_type=F32)
            acc = pv if acc is None else acc + pv
        av = acc[:, :ATTN_V_DIM] / acc[:, ATTN_V_DIM:]
        o = av[:tq] - lam * av[tq:]
        ms = jnp.mean(o * o, axis=-1, keepdims=True)
        o_ref[i * tq:(i + 1) * tq, cols[hh]] = (o * lax.rsqrt(ms + EPS) * sw_ref[...]
                                                * (1.0 - lam_init)).astype(BF16)

    units = [(hh, i) for hh in range(heads) for i in range(q_ref.shape[0] // tq)]
    e = probs(*units[0])
    for n, unit in enumerate(units):
        e_next = probs(*units[n + 1]) if n + 1 < len(units) else None
        attend(*unit, e)
        e = e_next


def _attn_call(lq, sw, q, ks, vs, *, batch, tq, heads, lam_init, casts=()):
    rows = q.shape[0]
    lq_len = rows // batch
    hd = ATTN_V_DIM * heads
    hsteps = ATTN_HEADS // heads
    n_steps = batch * hsteps
    kv_specs = [pl.BlockSpec((k.shape[0] // batch, hd), lambda b, h: (b, h)) for k in ks + vs]
    cast_in, cast_out, cast_shape = [], [], []
    for w, l in casts:
        _, r, c = w.shape
        n = n_steps
        while r % n or (r // n) % (2 * SUBLANES):
            n //= 2
        rep = n_steps // n
        cast_in.append(pl.BlockSpec((None, r // n, c),
                                    lambda b, h, l=l, rep=rep: (l, (b * hsteps + h) // rep, 0)))
        cast_out.append(pl.BlockSpec((r // n, c), lambda b, h, rep=rep: ((b * hsteps + h) // rep, 0)))
        cast_shape.append(jax.ShapeDtypeStruct((r, c), BF16))
    out = pl.pallas_call(
        functools.partial(_attn_kernel, n_kv=len(ks), n_cast=len(casts), tq=tq, lam_init=lam_init),
        grid=(batch, hsteps),
        in_specs=[
            _const_spec(lq.shape),
            _const_spec(sw.shape),
            pl.BlockSpec((lq_len, hd), lambda b, h: (b, h)),
        ] + kv_specs + cast_in,
        out_specs=[pl.BlockSpec((lq_len, hd), lambda b, h: (b, h))] + cast_out,
        out_shape=[jax.ShapeDtypeStruct((rows, ATTN_WIDTH), BF16)] + cast_shape,
        compiler_params=_params(2),
        name="attn",
    )(lq, sw, q, *ks, *vs, *[w for w, _ in casts])
    return out[0], out[1:]


def _shift_rows(a, d):
    n = a.shape[0]
    return pltpu.roll(a, (-d) % n, axis=0)


def _mix_ffn_kernel(h_ref, mod_ref, attn_ref, cp_ref, prev_ref, next_ref, icnt_ref, cw_ref, pw_ref, ps_ref,
                    wout_ref, nw_ref, wgu_ref, wd_ref, o_ref, *, tiles_per_seq):
    tm = h_ref.shape[0]
    cw = cp_ref.shape[1] // 3
    aw = attn_ref.shape[1]
    y = jnp.dot(attn_ref[...], wout_ref[:aw, :], preferred_element_type=F32)
    jj = pl.program_id(0) % tiles_per_seq
    keep_prev = (jj > 0).astype(F32)
    keep_next = (jj < tiles_per_seq - 1).astype(F32)
    ext = jnp.concatenate([prev_ref[:, cw:] * keep_prev, cp_ref[:, cw:], next_ref[:, cw:] * keep_next],
                          axis=0)
    cur = slice(HALO, HALO + tm)
    uc = ext[:, :cw]
    conv = cp_ref[:, :cw] * (_shift_rows(uc, -1)[cur] * cw_ref[0:1, :] + uc[cur] * cw_ref[1:2, :]
                             + _shift_rows(uc, 1)[cur] * cw_ref[2:3, :])
    y = y + jnp.dot(conv.astype(BF16), wout_ref[aw:aw + cw, :], preferred_element_type=F32)
    group = cw // len(POOL_WINDOWS)
    lane = lax.broadcasted_iota(jnp.int32, (tm, LANES), 1)
    pooled = []
    for s in range(cw // LANES):
        x = ext[:, cw + s * LANES:cw + (s + 1) * LANES]
        s2 = x + _shift_rows(x, -1)
        s4 = _shift_rows(s2, -1) + _shift_rows(s2, 1)
        if s == 0:
            lo, hi = s2, s4
        else:
            s8 = _shift_rows(s4, -2) + _shift_rows(s4, 2)
            s16 = _shift_rows(s8, -4) + _shift_rows(s8, 4)
            lo, hi = s8, s16
        pooled.append(jnp.where(lane < group, lo[cur], hi[cur]) * icnt_ref[:, s * LANES:(s + 1) * LANES]
                      - x[cur])
    pooled = jnp.concatenate(pooled, axis=1).astype(BF16)
    pool = jnp.dot(pooled, pw_ref[...], preferred_element_type=F32) * ps_ref[...]
    y = y + jnp.dot(pool.astype(BF16), wout_ref[aw + cw:, :], preferred_element_type=F32)
    h1 = h_ref[...] + mod_ref[5:6, :] * y
    o_ref[...] = _ffn_step(h1, mod_ref, 6, nw_ref, wgu_ref, wd_ref)


def _mix_ffn_call(h, mods, mod_row, attn, cp, icnt, conv_w, pool_w, pool_s, wout, nw, wgu, wd, *, tm):
    rows, d = h.shape
    aw = attn.shape[1]
    cw3 = cp.shape[1]
    d_ff = wd.shape[0]
    tiles_per_seq = icnt.shape[0] // tm
    hpt = tm // HALO
    last = rows // HALO - 1
    return pl.pallas_call(
        functools.partial(_mix_ffn_kernel, tiles_per_seq=tiles_per_seq),
        grid=(rows // tm,),
        in_specs=[
            pl.BlockSpec((tm, d), lambda j: (j, 0)),
            pl.BlockSpec((None, N_MOD, d), lambda j: (mod_row(j * tm), 0, 0)),
            pl.BlockSpec((tm, aw), lambda j: (j, 0)),
            pl.BlockSpec((tm, cw3), lambda j: (j, 0)),
            pl.BlockSpec((HALO, cw3), lambda j: (jnp.maximum(j * hpt - 1, 0), 0)),
            pl.BlockSpec((HALO, cw3), lambda j: (jnp.minimum((j + 1) * hpt, last), 0)),
            pl.BlockSpec((tm, icnt.shape[1]), lambda j: (j % tiles_per_seq, 0)),
            _const_spec(conv_w.shape),
            _const_spec(pool_w.shape),
            _const_spec(pool_s.shape),
            _const_spec(wout.shape),
            _const_spec((1, d)),
            _const_spec((d, 2 * d_ff)),
            _const_spec((d_ff, d)),
        ],
        out_specs=pl.BlockSpec((tm, d), lambda j: (j, 0)),
        out_shape=jax.ShapeDtypeStruct((rows, d), F32),
        compiler_params=_params(1),
        name="mix_ffn",
    )(h, mods, attn, cp, cp, cp, icnt, conv_w, pool_w, pool_s, wout, nw, wgu, wd)


def _rope_tables(n_tokens):
    pos = jnp.arange(n_tokens)
    row = (pos // GRID_W).astype(F32)
    col = (pos % GRID_W).astype(F32)
    nfreq = ROPE_PAIR
    inv = ROPE_BASE ** (-jnp.arange(nfreq, dtype=F32) / nfreq)
    ar, ac = row[:, None] * inv, col[:, None] * inv
    cos = jnp.concatenate([jnp.cos(ar), jnp.cos(ar), jnp.cos(ac), jnp.cos(ac)], axis=1)
    sin = jnp.concatenate([-jnp.sin(ar), jnp.sin(ar), -jnp.sin(ac), jnp.sin(ac)], axis=1)
    return jnp.tile(cos, (1, 2)), jnp.tile(sin, (1, 2))


def _pool_inv_counts(seq_len, group_dim):
    t = jnp.arange(seq_len)[:, None]
    half = jnp.repeat(jnp.array([w // 2 for w in POOL_WINDOWS]), group_dim)[None, :]
    cnt = jnp.minimum(t + half, seq_len) - jnp.maximum(t - half, 0)
    return 1.0 / cnt.astype(F32)


def _block_diag(blocks):
    g, n, _ = blocks.shape
    eye = jnp.eye(g, dtype=blocks.dtype)
    return (eye[:, None, :, None] * blocks[:, :, None, :]).reshape(g * n, g * n)


def kernel(x, c, ctx, c_ctx, norm_w, w_mod, b_mod, ffn1_w_gu, ffn1_w_down, ffn2_w_gu, ffn2_w_down,
           w_in, w_out, q_norm_w, k_norm_w, lambda_qk, subln_w, conv_w, pool_w, pool_scale):
    batch, seq, d = x.shape
    ctx_len = ctx.shape[1]
    depth = w_mod.shape[0]
    n_heads_maps = ATTN_WIDTH // ATTN_HEAD_DIM

    cc = jnp.concatenate([c, c_ctx[None]], axis=0)
    mods_all = _mods_call(cc, w_mod, b_mod).reshape(depth, batch + 1, N_MOD, d)

    wgu1, wd1, win = (w[0].astype(BF16) for w in (ffn1_w_gu, ffn1_w_down, w_in))
    gmat = _block_diag(jnp.ones((MXU_DIM // ATTN_HEAD_DIM, ATTN_HEAD_DIM, ATTN_HEAD_DIM), BF16))
    cos, sin = _rope_tables(seq)
    icnt, icnt_c = (_pool_inv_counts(n, pool_w.shape[-1]) for n in (seq, ctx_len))

    tm_ffn, tm_lat, tm_ctx, tq_lat = 1024, 512, 256, 256
    lat_row = lambda r: r // seq
    ctx_row = lambda r: batch

    h = x.reshape(batch * seq, d)
    hc = ctx.reshape(batch * ctx_len, d)
    for l in range(depth):
        ctx_out = l < depth - 1
        lam_init = 0.8 - 0.6 * math.exp(-0.3 * l)
        mods = mods_all[l]
        nw = norm_w[l][:, None, :]
        qw = jnp.tile(q_norm_w[l], n_heads_maps)[None]
        kw = jnp.tile(k_norm_w[l], n_heads_maps)[None]
        pw = _block_diag(pool_w[l]).astype(BF16)
        ps = pool_scale[l][None]
        sw = subln_w[l][None]

        h = _ffn_call(h, mods, lat_row, nw[0], wgu1, wd1, i0=0, tm=tm_ffn)
        hc = _ffn_call(hc, mods, ctx_row, nw[0], wgu1, wd1, i0=0, tm=tm_ffn)

        q, k, v, cp = _proj_call(h, mods, lat_row, nw[1], win, gmat, qw, kw, cos, sin,
                                 tm=tm_ffn, tiles_per_seq=seq // tm_ffn)
        qc, kc, vc, cpc = _proj_call(hc, mods, ctx_row, nw[1], win, gmat, qw, kw, None, None,
                                     tm=tm_ffn, tiles_per_seq=1)

        casts = [(ffn2_w_gu, l), (ffn2_w_down, l), (w_out, l)]
        if ctx_out:
            casts += [(ffn1_w_gu, l + 1), (ffn1_w_down, l + 1), (w_in, l + 1)]
        attn, (wgu2, wd2, wout, *nxt) = _attn_call(lambda_qk[l], sw, q, [k, kc], [v, vc], batch=batch,
                                                   tq=tq_lat, heads=1, lam_init=lam_init, casts=casts)
        h = _mix_ffn_call(h, mods, lat_row, attn, cp, icnt, conv_w[l], pw, ps, wout, nw[2], wgu2, wd2,
                          tm=tm_lat)
        if ctx_out:
            attn_c, _ = _attn_call(lambda_qk[l], sw, qc, [kc], [vc], batch=batch, tq=ctx_len,
                                   heads=ATTN_HEADS, lam_init=lam_init)
            hc = _mix_ffn_call(hc, mods, ctx_row, attn_c, cpc, icnt_c, conv_w[l], pw, ps, wout, nw[2], wgu2,
                               wd2, tm=tm_ctx)
            wgu1, wd1, win = nxt
    return h.reshape(batch, seq, d)
```

```python
import functools
import math

import jax
import jax.numpy as jnp
from jax import lax
from jax.experimental import pallas as pl
from jax.experimental.pallas import tpu as pltpu

F32 = jnp.float32
BF16 = jnp.bfloat16

GRID_W = 64
ATTN_HEADS = 4
ATTN_HEAD_DIM = 64
ATTN_V_DIM = 2 * ATTN_HEAD_DIM
ATTN_WIDTH = ATTN_HEADS * ATTN_V_DIM
POOL_WINDOWS = (2, 4, 8, 16)
ROPE_BASE = 10000.0
N_MOD = 9
EPS = 1e-6
ROPE_PAIR = ATTN_HEAD_DIM // 4
Q_SCALE = ATTN_HEAD_DIM ** -0.5 * math.log2(math.e)

LANES = 128
SUBLANES = 8
HALO = SUBLANES
MXU_DIM = 256
VMEM_LIMIT = 56 * 1024 * 1024


def _const_spec(shape):
    zeros = (0,) * len(shape)
    return pl.BlockSpec(shape, lambda *_: zeros, pipeline_mode=pl.Buffered(1))


def _params(n_axes):
    return pltpu.CompilerParams(dimension_semantics=("arbitrary",) * n_axes,
                                vmem_limit_bytes=VMEM_LIMIT)


def _rms_mod(x, nw, shift, scale):
    ms = jnp.mean(x * x, axis=-1, keepdims=True)
    y = x * lax.rsqrt(ms + EPS) * nw
    return y * (1.0 + scale) + shift


def _silu(x):
    return x * jax.nn.sigmoid(x)


def _mods_kernel(c_ref, w_ref, b_ref, o_ref):
    s = _silu(c_ref[...]).astype(BF16)
    o_ref[...] = jnp.dot(s, w_ref[...].astype(BF16), preferred_element_type=F32) + b_ref[...]


def _mods_call(cc, w_mod, b_mod):
    depth, d, n = w_mod.shape
    rows = cc.shape[0]
    tn = n // 2
    return pl.pallas_call(
        _mods_kernel,
        grid=(depth, n // tn),
        in_specs=[
            pl.BlockSpec((rows, d), lambda l, j: (0, 0)),
            pl.BlockSpec((None, d, tn), lambda l, j: (l, 0, j)),
            pl.BlockSpec((None, 1, tn), lambda l, j: (l, 0, j)),
        ],
        out_specs=pl.BlockSpec((None, rows, tn), lambda l, j: (l, 0, j)),
        out_shape=jax.ShapeDtypeStruct((depth, rows, n), F32),
        compiler_params=_params(2),
        name="mods",
    )(cc, w_mod, b_mod.reshape(depth, 1, n))


def _ffn_step(x, mod_ref, i0, nw_ref, wgu_ref, wd_ref):
    d_ff = wd_ref.shape[0]
    xn = _rms_mod(x, nw_ref[...], mod_ref[i0:i0 + 1, :], mod_ref[i0 + 1:i0 + 2, :]).astype(BF16)
    split = (d_ff // MXU_DIM + 1) // 2 * MXU_DIM
    acts = []
    for a0, a1 in ((0, split), (split, d_ff)):
        g = jnp.dot(xn, wgu_ref[:, a0:a1], preferred_element_type=F32)
        u = jnp.dot(xn, wgu_ref[:, d_ff + a0:d_ff + a1], preferred_element_type=F32)
        acts.append((_silu(g) * u).astype(BF16))
    f = (jnp.dot(acts[0], wd_ref[:split, :], preferred_element_type=F32)
         + jnp.dot(acts[1], wd_ref[split:, :], preferred_element_type=F32))
    return x + (0.5 * mod_ref[i0 + 2:i0 + 3, :]) * f


def _ffn_kernel(h_ref, mod_ref, nw_ref, wgu_ref, wd_ref, o_ref, *, i0):
    o_ref[...] = _ffn_step(h_ref[...], mod_ref, i0, nw_ref, wgu_ref, wd_ref)


def _ffn_call(h, mods, mod_row, nw, wgu, wd, *, i0, tm):
    rows, d = h.shape
    d_ff = wd.shape[0]
    return pl.pallas_call(
        functools.partial(_ffn_kernel, i0=i0),
        grid=(rows // tm,),
        in_specs=[
            pl.BlockSpec((tm, d), lambda j: (j, 0)),
            pl.BlockSpec((None, N_MOD, d), lambda j: (mod_row(j * tm), 0, 0)),
            _const_spec((1, d)),
            _const_spec((d, 2 * d_ff)),
            _const_spec((d_ff, d)),
        ],
        out_specs=pl.BlockSpec((tm, d), lambda j: (j, 0)),
        out_shape=jax.ShapeDtypeStruct((rows, d), F32),
        compiler_params=_params(1),
        name="ffn",
    )(h, mods, nw, wgu, wd)


def _swap_halves(z):
    lane = lax.broadcasted_iota(jnp.int32, z.shape, 1)
    up = pltpu.roll(z, LANES - ROPE_PAIR, axis=1)
    down = pltpu.roll(z, ROPE_PAIR, axis=1)
    return jnp.where((lane & (2 * ROPE_PAIR - 1)) < ROPE_PAIR, up, down)


def _head_norm(z, w_ref, g_ref):
    outs = []
    for s in range(z.shape[1] // MXU_DIM):
        zs = z[:, s * MXU_DIM:(s + 1) * MXU_DIM]
        ss = jnp.dot((zs * zs).astype(BF16), g_ref[...], preferred_element_type=F32)
        outs.append(zs * lax.rsqrt(ss * (1.0 / ATTN_HEAD_DIM) + EPS)
                    * w_ref[:, s * MXU_DIM:(s + 1) * MXU_DIM])
    return outs


def _rope(z, cos, sin):
    outs = []
    for s in range(z.shape[1] // LANES):
        zs = z[:, s * LANES:(s + 1) * LANES]
        outs.append(zs * cos + _swap_halves(zs) * sin)
    return jnp.concatenate(outs, axis=1)


def _proj_kernel(h_ref, mod_ref, nw_ref, win_ref, g_ref, qw_ref, kw_ref, *rest, rope):
    if rope:
        cos_ref, sin_ref, q_ref, k_ref, v_ref, cp_ref = rest
    else:
        q_ref, k_ref, v_ref, cp_ref = rest
    aw = ATTN_WIDTH
    cw = (win_ref.shape[1] - 3 * aw) // 4
    u = _rms_mod(h_ref[...], nw_ref[...], mod_ref[3:4, :], mod_ref[4:5, :]).astype(BF16)
    pqk = jnp.dot(u, win_ref[:, :2 * aw], preferred_element_type=F32)
    v_ref[...] = jnp.dot(u, win_ref[:, 2 * aw:3 * aw], preferred_element_type=F32).astype(BF16)
    normed = [_head_norm(pqk[:, :aw], qw_ref, g_ref), _head_norm(pqk[:, aw:], kw_ref, g_ref)]
    p = jnp.dot(u, win_ref[:, 3 * aw:], preferred_element_type=F32)
    for slabs, o_ref, sc in zip(normed, (q_ref, k_ref), (Q_SCALE, None)):
        if rope:
            slabs = [_rope(zs, cos_ref[...], sin_ref[...]) for zs in slabs]
        for s, zs in enumerate(slabs):
            if sc is not None:
                zs = zs * sc
            o_ref[:, s * MXU_DIM:(s + 1) * MXU_DIM] = zs.astype(BF16)
    cp_ref[:, :cw] = p[:, :cw]
    cp_ref[:, cw:2 * cw] = p[:, cw:2 * cw] * p[:, 2 * cw:3 * cw]
    cp_ref[:, 2 * cw:] = p[:, 3 * cw:]


def _proj_call(h, mods, mod_row, nw, win, gmat, qw, kw, cos, sin, *, tm, tiles_per_seq):
    rows, d = h.shape
    ncol = win.shape[1]
    aw = ATTN_WIDTH
    cw = (ncol - 3 * aw) // 4
    rope = cos is not None
    in_specs = [
        pl.BlockSpec((tm, d), lambda j: (j, 0)),
        pl.BlockSpec((None, N_MOD, d), lambda j: (mod_row(j * tm), 0, 0)),
        _const_spec((1, d)),
        _const_spec((d, ncol)),
        _const_spec((MXU_DIM, MXU_DIM)),
        _const_spec((1, aw)),
        _const_spec((1, aw)),
    ]
    args = [h, mods, nw, win, gmat, qw, kw]
    if rope:
        in_specs += [pl.BlockSpec((tm, LANES), lambda j: (j % tiles_per_seq, 0))] * 2
        args += [cos, sin]
    return pl.pallas_call(
        functools.partial(_proj_kernel, rope=rope),
        grid=(rows // tm,),
        in_specs=in_specs,
        out_specs=[pl.BlockSpec((tm, aw), lambda j: (j, 0))] * 3
        + [pl.BlockSpec((tm, 3 * cw), lambda j: (j, 0))],
        out_shape=[jax.ShapeDtypeStruct((rows, aw), BF16)] * 3
        + [jax.ShapeDtypeStruct((rows, 3 * cw), F32)],
        compiler_params=_params(1),
        name="proj",
    )(*args)


def _attn_kernel(lq_ref, sw_ref, q_ref, *rest, n_kv, n_cast, tq, lam_init):
    k_refs, v_refs = rest[:n_kv], rest[n_kv:2 * n_kv]
    cast_in, o_ref, cast_out = rest[2 * n_kv:2 * n_kv + n_cast], rest[2 * n_kv + n_cast], rest[2 * n_kv + n_cast + 1:]
    for src, dst in zip(cast_in, cast_out):
        dst[...] = src[...].astype(BF16)
    lq = lq_ref[...]
    lam = (jnp.exp(jnp.sum(lq[0:1] * lq[1:2], axis=-1, keepdims=True))
           - jnp.exp(jnp.sum(lq[2:3] * lq[3:4], axis=-1, keepdims=True)) + lam_init)
    lane = lax.broadcasted_iota(jnp.int32, (tq, ATTN_V_DIM), 1)
    nt = (((1,), (1,)), ((), ()))
    heads = q_ref.shape[1] // ATTN_V_DIM
    cols = [slice(hh * ATTN_V_DIM, (hh + 1) * ATTN_V_DIM) for hh in range(heads)]
    va = [[jnp.concatenate([v[:, c], jnp.ones((v.shape[0], ATTN_V_DIM), BF16)], axis=1) for v in v_refs]
          for c in cols]

    def probs(hh, i):
        q = q_ref[i * tq:(i + 1) * tq, cols[hh]]
        q2 = jnp.concatenate([jnp.where(lane < ATTN_HEAD_DIM, q, jnp.zeros_like(q)),
                              jnp.where(lane >= ATTN_HEAD_DIM, q, jnp.zeros_like(q))], axis=0)
        s = [lax.dot_general(q2, k[:, cols[hh]], nt, preferred_element_type=F32) for k in k_refs]
        m = functools.reduce(jnp.maximum, [jnp.max(x, axis=-1, keepdims=True) for x in s])
        return [jnp.exp2(x - m).astype(BF16) for x in s]

    def attend(hh, i, e):
        acc = None
        for x, v in zip(e, va[hh]):
            pv = jnp.dot(x, v, preferred_element_type=F32)
            acc = pv if acc is None else acc + pv
        av = acc[:, :ATTN_V_DIM] / acc[:, ATTN_V_DIM:]
        o = av[:tq] - lam * av[tq:]
        ms = jnp.mean(o * o, axis=-1, keepdims=True)
        o_ref[i * tq:(i + 1) * tq, cols[hh]] = (o * lax.rsqrt(ms + EPS) * sw_ref[...]
                                                * (1.0 - lam_init)).astype(BF16)

    units = [(hh, i) for hh in range(heads) for i in range(q_ref.shape[0] // tq)]
    e = probs(*units[0])
    for n, unit in enumerate(units):
        e_next = probs(*units[n + 1]) if n + 1 < len(units) else None
        attend(*unit, e)
        e = e_next


def _attn_call(lq, sw, q, ks, vs, *, batch, tq, heads, lam_init, casts=()):
    rows = q.shape[0]
    lq_len = rows // batch
    hd = ATTN_V_DIM * heads
    hsteps = ATTN_HEADS // heads
    n_steps = batch * hsteps
    kv_specs = [pl.BlockSpec((k.shape[0] // batch, hd), lambda b, h: (b, h)) for k in ks + vs]
    cast_in, cast_out, cast_shape = [], [], []
    for w, l in casts:
        _, r, c = w.shape
        n = n_steps
        while r % n or (r // n) % (2 * SUBLANES):
            n //= 2
        rep = n_steps // n
        cast_in.append(pl.BlockSpec((None, r // n, c),
                                    lambda b, h, l=l, rep=rep: (l, (b * hsteps + h) // rep, 0)))
        cast_out.append(pl.BlockSpec((r // n, c), lambda b, h, rep=rep: ((b * hsteps + h) // rep, 0)))
        cast_shape.append(jax.ShapeDtypeStruct((r, c), BF16))
    out = pl.pallas_call(
        functools.partial(_attn_kernel, n_kv=len(ks), n_cast=len(casts), tq=tq, lam_init=lam_init),
        grid=(batch, hsteps),
        in_specs=[
            _const_spec(lq.shape),
            _const_spec(sw.shape),
            pl.BlockSpec((lq_len, hd), lambda b, h: (b, h)),
        ] + kv_specs + cast_in,
        out_specs=[pl.BlockSpec((lq_len, hd), lambda b, h: (b, h))] + cast_out,
        out_shape=[jax.ShapeDtypeStruct((rows, ATTN_WIDTH), BF16)] + cast_shape,
        compiler_params=_params(2),
        name="attn",
    )(lq, sw, q, *ks, *vs, *[w for w, _ in casts])
    return out[0], out[1:]


def _shift_rows(a, d):
    n = a.shape[0]
    return pltpu.roll(a, (-d) % n, axis=0)


def _mix_ffn_kernel(h_ref, mod_ref, attn_ref, cp_ref, prev_ref, next_ref, icnt_ref, cw_ref, pw_ref, ps_ref,
                    wout_ref, nw_ref, wgu_ref, wd_ref, o_ref, *, tiles_per_seq):
    tm = h_ref.shape[0]
    cw = cp_ref.shape[1] // 3
    aw = attn_ref.shape[1]
    y = jnp.dot(attn_ref[...], wout_ref[:aw, :], preferred_element_type=F32)
    jj = pl.program_id(0) % tiles_per_seq
    keep_prev = (jj > 0).astype(F32)
    keep_next = (jj < tiles_per_seq - 1).astype(F32)
    ext = jnp.concatenate([prev_ref[:, cw:] * keep_prev, cp_ref[:, cw:], next_ref[:, cw:] * keep_next],
                          axis=0)
    cur = slice(HALO, HALO + tm)
    uc = ext[:, :cw]
    conv = cp_ref[:, :cw] * (_shift_rows(uc, -1)[cur] * cw_ref[0:1, :] + uc[cur] * cw_ref[1:2, :]
                             + _shift_rows(uc, 1)[cur] * cw_ref[2:3, :])
    y = y + jnp.dot(conv.astype(BF16), wout_ref[aw:aw + cw, :], preferred_element_type=F32)
    group = cw // len(POOL_WINDOWS)
    lane = lax.broadcasted_iota(jnp.int32, (tm, LANES), 1)
    pooled = []
    for s in range(cw // LANES):
        x = ext[:, cw + s * LANES:cw + (s + 1) * LANES]
        s2 = x + _shift_rows(x, -1)
        s4 = _shift_rows(s2, -1) + _shift_rows(s2, 1)
        if s == 0:
            lo, hi = s2, s4
        else:
            s8 = _shift_rows(s4, -2) + _shift_rows(s4, 2)
            s16 = _shift_rows(s8, -4) + _shift_rows(s8, 4)
            lo, hi = s8, s16
        pooled.append(jnp.where(lane < group, lo[cur], hi[cur]) * icnt_ref[:, s * LANES:(s + 1) * LANES]
                      - x[cur])
    pooled = jnp.concatenate(pooled, axis=1).astype(BF16)
    pool = jnp.dot(pooled, pw_ref[...], preferred_element_type=F32) * ps_ref[...]
    y = y + jnp.dot(pool.astype(BF16), wout_ref[aw + cw:, :], preferred_element_type=F32)
    h1 = h_ref[...] + mod_ref[5:6, :] * y
    o_ref[...] = _ffn_step(h1, mod_ref, 6, nw_ref, wgu_ref, wd_ref)


def _mix_ffn_call(h, mods, mod_row, attn, cp, icnt, conv_w, pool_w, pool_s, wout, nw, wgu, wd, *, tm):
    rows, d = h.shape
    aw = attn.shape[1]
    cw3 = cp.shape[1]
    d_ff = wd.shape[0]
    tiles_per_seq = icnt.shape[0] // tm
    hpt = tm // HALO
    last = rows // HALO - 1
    return pl.pallas_call(
        functools.partial(_mix_ffn_kernel, tiles_per_seq=tiles_per_seq),
        grid=(rows // tm,),
        in_specs=[
            pl.BlockSpec((tm, d), lambda j: (j, 0)),
            pl.BlockSpec((None, N_MOD, d), lambda j: (mod_row(j * tm), 0, 0)),
            pl.BlockSpec((tm, aw), lambda j: (j, 0)),
            pl.BlockSpec((tm, cw3), lambda j: (j, 0)),
            pl.BlockSpec((HALO, cw3), lambda j: (jnp.maximum(j * hpt - 1, 0), 0)),
            pl.BlockSpec((HALO, cw3), lambda j: (jnp.minimum((j + 1) * hpt, last), 0)),
            pl.BlockSpec((tm, icnt.shape[1]), lambda j: (j % tiles_per_seq, 0)),
            _const_spec(conv_w.shape),
            _const_spec(pool_w.shape),
            _const_spec(pool_s.shape),
            _const_spec(wout.shape),
            _const_spec((1, d)),
            _const_spec((d, 2 * d_ff)),
            _const_spec((d_ff, d)),
        ],
        out_specs=pl.BlockSpec((tm, d), lambda j: (j, 0)),
        out_shape=jax.ShapeDtypeStruct((rows, d), F32),
        compiler_params=_params(1),
        name="mix_ffn",
    )(h, mods, attn, cp, cp, cp, icnt, conv_w, pool_w, pool_s, wout, nw, wgu, wd)


def _rope_tables(n_tokens):
    pos = jnp.arange(n_tokens)
    row = (pos // GRID_W).astype(F32)
    col = (pos % GRID_W).astype(F32)
    nfreq = ROPE_PAIR
    inv = ROPE_BASE ** (-jnp.arange(nfreq, dtype=F32) / nfreq)
    ar, ac = row[:, None] * inv, col[:, None] * inv
    cos = jnp.concatenate([jnp.cos(ar), jnp.cos(ar), jnp.cos(ac), jnp.cos(ac)], axis=1)
    sin = jnp.concatenate([-jnp.sin(ar), jnp.sin(ar), -jnp.sin(ac), jnp.sin(ac)], axis=1)
    return jnp.tile(cos, (1, 2)), jnp.tile(sin, (1, 2))


def _pool_inv_counts(seq_len, group_dim):
    t = jnp.arange(seq_len)[:, None]
    half = jnp.repeat(jnp.array([w // 2 for w in POOL_WINDOWS]), group_dim)[None, :]
    cnt = jnp.minimum(t + half, seq_len) - jnp.maximum(t - half, 0)
    return 1.0 / cnt.astype(F32)


def _block_diag(blocks):
    g, n, _ = blocks.shape
    eye = jnp.eye(g, dtype=blocks.dtype)
    return (eye[:, None, :, None] * blocks[:, :, None, :]).reshape(g * n, g * n)


def kernel(x, c, ctx, c_ctx, norm_w, w_mod, b_mod, ffn1_w_gu, ffn1_w_down, ffn2_w_gu, ffn2_w_down,
           w_in, w_out, q_norm_w, k_norm_w, lambda_qk, subln_w, conv_w, pool_w, pool_scale):
    batch, seq, d = x.shape
    ctx_len = ctx.shape[1]
    depth = w_mod.shape[0]
    n_heads_maps = ATTN_WIDTH // ATTN_HEAD_DIM

    cc = jnp.concatenate([c, c_ctx[None]], axis=0)
    mods_all = _mods_call(cc, w_mod, b_mod).reshape(depth, batch + 1, N_MOD, d)

    wgu1, wd1, win = (w[0].astype(BF16) for w in (ffn1_w_gu, ffn1_w_down, w_in))
    gmat = _block_diag(jnp.ones((MXU_DIM // ATTN_HEAD_DIM, ATTN_HEAD_DIM, ATTN_HEAD_DIM), BF16))
    cos, sin = _rope_tables(seq)
    icnt, icnt_c = (_pool_inv_counts(n, pool_w.shape[-1]) for n in (seq, ctx_len))

    tm_ffn, tm_lat, tm_ctx, tq_lat = 1024, 512, 256, 256
    lat_row = lambda r: r // seq
    ctx_row = lambda r: batch

    h = x.reshape(batch * seq, d)
    hc = ctx.reshape(batch * ctx_len, d)
    for l in range(depth):
        ctx_out = l < depth - 1
        lam_init = 0.8 - 0.6 * math.exp(-0.3 * l)
        mods = mods_all[l]
        nw = norm_w[l][:, None, :]
        qw = jnp.tile(q_norm_w[l], n_heads_maps)[None]
        kw = jnp.tile(k_norm_w[l], n_heads_maps)[None]
        pw = _block_diag(pool_w[l]).astype(BF16)
        ps = pool_scale[l][None]
        sw = subln_w[l][None]

        h = _ffn_call(h, mods, lat_row, nw[0], wgu1, wd1, i0=0, tm=tm_ffn)
        hc = _ffn_call(hc, mods, ctx_row, nw[0], wgu1, wd1, i0=0, tm=tm_ffn)

        q, k, v, cp = _proj_call(h, mods, lat_row, nw[1], win, gmat, qw, kw, cos, sin,
                                 tm=tm_ffn, tiles_per_seq=seq // tm_ffn)
        qc, kc, vc, cpc = _proj_call(hc, mods, ctx_row, nw[1], win, gmat, qw, kw, None, None,
                                     tm=tm_ffn, tiles_per_seq=1)

        casts = [(ffn2_w_gu, l), (ffn2_w_down, l), (w_out, l)]
        if ctx_out:
            casts += [(ffn1_w_gu, l + 1), (ffn1_w_down, l + 1), (w_in, l + 1)]
        attn, (wgu2, wd2, wout, *nxt) = _attn_call(lambda_qk[l], sw, q, [k, kc], [v, vc], batch=batch,
                                                   tq=tq_lat, heads=1, lam_init=lam_init, casts=casts)
        h = _mix_ffn_call(h, mods, lat_row, attn, cp, icnt, conv_w[l], pw, ps, wout, nw[2], wgu2, wd2,
                          tm=tm_lat)
        if ctx_out:
            attn_c, _ = _attn_call(lambda_qk[l], sw, qc, [kc], [vc], batch=batch, tq=ctx_len,
                                   heads=ATTN_HEADS, lam_init=lam_init)
            hc = _mix_ffn_call(hc, mods, ctx_row, attn_c, cpc, icnt_c, conv_w[l], pw, ps, wout, nw[2], wgu2,
                               wd2, tm=tm_ctx)
            wgu1, wd1, win = nxt
    return h.reshape(batch, seq, d)
```
